```python
import numpy as np
import jax
import jax.numpy as jnp
from jax import lax

D_MODEL = 1024
BATCH = 32
SEQ = 2048
DEPTH = 4

HEAD_DIM = 64
QBLOCK = 128
ROPE_THETA = 10000.0
PLE_DIM = 256
LN_EPS = 1e-5
RMS_EPS = 1e-6
NEG = -1e30
BIG = 1e30

A_HEADS = 8
IDX_HEADS = 8
IDX_DIM = 64
DSA_TOPK_MAX = 256

B_HEADS = 8
CMP_LEN = 32
CMP_STRIDE = 16
SEL_LEN = 32
SEL_BLOCKS_MAX = 8
WINDOW = 512

C_HEADS = 16
FORGET_BIAS_INIT = 3.0

N_EXPERTS = 64
TOP_K = 8
N_GROUPS = 8
TOPK_GROUPS = 4
D_EXPERT = 256
ROUTED_SCALE = 2.5
EXPERT_BLOCK = 256

N_EVEN = (DEPTH + 1) // 2
N_ODD = DEPTH // 2
ALPHA = (2.0 * DEPTH) ** 0.25
BETA = (8.0 * DEPTH) ** -0.25

MIX_A = A_HEADS * HEAD_DIM
MIX_B = B_HEADS * HEAD_DIM
MIX_EVEN = MIX_A + MIX_B
MIX_ODD = C_HEADS * HEAD_DIM
EVEN_SPLITS = (MIX_A, HEAD_DIM, HEAD_DIM, IDX_HEADS * IDX_DIM, IDX_DIM, IDX_HEADS, MIX_B, 6 * HEAD_DIM, 3 * B_HEADS)
EVEN_COLS = sum(EVEN_SPLITS)
ODD_SPLITS = (MIX_ODD, MIX_ODD, MIX_ODD, C_HEADS)
ODD_COLS = sum(ODD_SPLITS)

kernel_name = 'hybrid_dsa_nsa_fox_moe_trunk'


def split_cols(h, sizes):
    cuts = [int(c) for c in np.cumsum(sizes)[:-1]]
    return jnp.split(h, cuts, axis=-1)


def rope_tables(pos):
    inv = 1.0 / (ROPE_THETA ** (np.arange(0, HEAD_DIM, 2, dtype=np.float32) / HEAD_DIM))
    ang = pos.astype(jnp.float32)[:, None] * jnp.asarray(inv, dtype=jnp.float32)[None, :]
    return jnp.cos(ang), jnp.sin(ang)


def apply_rope(x, cos, sin):
    if x.ndim == 4:
        cos = cos[:, None, :]
        sin = sin[:, None, :]
    half = x.shape[-1] // 2
    xf = x.astype(jnp.float32)
    x1, x2 = xf[..., :half], xf[..., half:]
    return jnp.concatenate([x1 * cos - x2 * sin, x2 * cos + x1 * sin], axis=-1).astype(x.dtype)


def layer_norm(x, g, b):
    xf = x.astype(jnp.float32)
    mu = jnp.mean(xf, axis=-1, keepdims=True)
    var = jnp.mean(jnp.square(xf - mu), axis=-1, keepdims=True)
    return ((xf - mu) * lax.rsqrt(var + LN_EPS)).astype(x.dtype) * g + b


def rms_norm(x, g):
    xf = x.astype(jnp.float32)
    return (xf * lax.rsqrt(jnp.mean(jnp.square(xf), axis=-1, keepdims=True) + RMS_EPS)).astype(x.dtype) * g


def masked_softmax(s, valid):
    p = jax.nn.softmax(jnp.where(valid, s, NEG), axis=-1)
    return jnp.where(valid, p, 0.0)


def to_blocks(a):
    b, s = a.shape[:2]
    return jnp.swapaxes(a.reshape((b, s // QBLOCK, QBLOCK) + a.shape[2:]), 0, 1)


def from_blocks(a):
    nb, b = a.shape[:2]
    return jnp.swapaxes(a, 0, 1).reshape((b, nb * QBLOCK) + a.shape[3:])


def dsa_attention(q, k, v, q_idx, k_idx, w_idx, pos):
    bsz, s_len = q.shape[:2]
    k_sel = min(DSA_TOPK_MAX, s_len // 4)
    b_idx = jnp.arange(bsz)[:, None, None]
    idx_scale = (IDX_HEADS * IDX_DIM) ** -0.5
    scale = HEAD_DIM ** -0.5

    def block(args):
        qb, qib, wb, t = args
        logits = jnp.einsum('bqhd,bkd->bqhk', qib, k_idx, preferred_element_type=jnp.float32)
        score = jnp.einsum('bqhk,bqh->bqk', jax.nn.relu(logits), wb.astype(jnp.float32)) * idx_scale
        causal = pos[None, :] <= t[:, None]
        score = jnp.where(causal[None], score, NEG)
        _, sel = lax.top_k(score, k_sel)
        ks = k[b_idx, sel]
        vs = v[b_idx, sel]
        valid = (sel <= t[None, :, None])[:, :, None, :]
        s = jnp.einsum('bqhd,bqkd->bqhk', qb, ks, preferred_element_type=jnp.float32) * scale
        pr = masked_softmax(s, valid)
        return jnp.einsum('bqhk,bqkd->bqhd', pr.astype(vs.dtype), vs)

    out = lax.map(block, (to_blocks(q), to_blocks(q_idx), to_blocks(w_idx), pos.reshape(-1, QBLOCK)))
    return from_blocks(out)


def nsa_attention(q, k_cmp, v_cmp, k_slc, v_slc, k_win, v_win, gate_logits,
                  pos_k, w1_k, w2_k, pos_v, w1_v, w2_v, pos, cos, sin):
    bsz, s_len = q.shape[:2]
    n_cmp = (s_len - CMP_LEN) // CMP_STRIDE + 1
    c_start = np.arange(n_cmp) * CMP_STRIDE
    cidx = c_start[:, None] + np.arange(CMP_LEN)[None, :]
    c_last = c_start + CMP_LEN - 1

    def compress(kv, pe, w1, w2):
        blk = (kv[:, cidx] + pe).reshape(bsz, n_cmp, CMP_LEN * HEAD_DIM)
        return jax.nn.silu(blk @ w1) @ w2

    kc = apply_rope(compress(k_cmp, pos_k, w1_k, w2_k), cos[c_last], sin[c_last])
    vc = compress(v_cmp, pos_v, w1_v, w2_v)
    n_sb = s_len // SEL_LEN
    n_sel = min(SEL_BLOCKS_MAX, n_sb)
    s_start = np.arange(n_sb) * SEL_LEN
    ov = np.clip(np.minimum(c_start[:, None] + CMP_LEN, s_start[None, :] + SEL_LEN)
                 - np.maximum(c_start[:, None], s_start[None, :]), 0, None)
    overlap = jnp.asarray(ov / CMP_LEN, dtype=jnp.float32)
    kw_pad = jnp.pad(k_win, ((0, 0), (WINDOW, 0), (0, 0)))
    vw_pad = jnp.pad(v_win, ((0, 0), (WINDOW, 0), (0, 0)))
    b_idx = jnp.arange(bsz)[:, None, None]
    c_last_j = jnp.asarray(c_last)
    s_start_j = jnp.asarray(s_start)
    blk_ids = jnp.arange(n_sb)
    scale = HEAD_DIM ** -0.5

    def block(args):
        qb, gb, t = args
        s_c = jnp.einsum('bqhd,bcd->bqhc', qb, kc, preferred_element_type=jnp.float32) * scale
        valid_c = (c_last_j[None, :] <= t[:, None])[None, :, None, :]
        p_c = masked_softmax(s_c, valid_c)
        o_c = jnp.einsum('bqhc,bcd->bqhd', p_c.astype(vc.dtype), vc, preferred_element_type=jnp.float32)
        imp = jnp.einsum('bqhc,cj->bqj', p_c, overlap)
        cur = t // SEL_LEN
        forced = (blk_ids[None, :] == 0) | (blk_ids[None, :] == cur[:, None]) | (blk_ids[None, :] == cur[:, None] - 1)
        future = s_start_j[None, :] > t[:, None]
        imp = jnp.where(forced[None], BIG, jnp.where(future[None], NEG, imp))
        _, sb = lax.top_k(imp, n_sel)
        tok = (sb[..., None] * SEL_LEN + jnp.arange(SEL_LEN)).reshape(bsz, QBLOCK, n_sel * SEL_LEN)
        ks = k_slc[b_idx, tok]
        vs = v_slc[b_idx, tok]
        valid_s = (tok <= t[None, :, None])[:, :, None, :]
        s_s = jnp.einsum('bqhd,bqkd->bqhk', qb, ks, preferred_element_type=jnp.float32) * scale
        o_s = jnp.einsum('bqhk,bqkd->bqhd', masked_softmax(s_s, valid_s).astype(vs.dtype), vs,
                         preferred_element_type=jnp.float32)
        q0 = t[0]
        kw = lax.dynamic_slice_in_dim(kw_pad, q0, WINDOW + QBLOCK, axis=1)
        vw = lax.dynamic_slice_in_dim(vw_pad, q0, WINDOW + QBLOCK, axis=1)
        kpos = q0 - WINDOW + jnp.arange(WINDOW + QBLOCK)
        valid_w = ((kpos[None, :] <= t[:, None]) & (kpos[None, :] > t[:, None] - WINDOW)
                   & (kpos[None, :] >= 0))[None, :, None, :]
        s_w = jnp.einsum('bqhd,bkd->bqhk', qb, kw, preferred_element_type=jnp.float32) * scale
        o_w = jnp.einsum('bqhk,bkd->bqhd', masked_softmax(s_w, valid_w).astype(vw.dtype), vw,
                         preferred_element_type=jnp.float32)
        g = jax.nn.sigmoid(gb.astype(jnp.float32))
        o = g[..., 0:1] * o_c + g[..., 1:2] * o_s + g[..., 2:3] * o_w
        return o.astype(qb.dtype)

    out = lax.map(block, (to_blocks(q), to_blocks(gate_logits), pos.reshape(-1, QBLOCK)))
    return from_blocks(out)


def fox_attention(q, k, v, log_f, pos):
    dcum = jnp.cumsum(log_f, axis=1)
    d_keys = jnp.transpose(dcum, (0, 2, 1))
    scale = HEAD_DIM ** -0.5

    def block(args):
        qb, db, t = args
        s = jnp.einsum('bqhd,bkhd->bhqk', qb, k, preferred_element_type=jnp.float32) * scale
        s = s + jnp.transpose(db, (0, 2, 1))[..., None] - d_keys[:, :, None, :]
        valid = (pos[None, :] <= t[:, None])[None, None]
        p = masked_softmax(s, valid)
        return jnp.einsum('bhqk,bkhd->bqhd', p.astype(v.dtype), v)

    out = lax.map(block, (to_blocks(q), to_blocks(dcum), pos.reshape(-1, QBLOCK)))
    return from_blocks(out)


def even_mixer(x, w_in, b_gate, pos_k, w1_k, w2_k, pos_v, w1_v, w2_v, w_o, pos, cos, sin):
    bsz, s_len, _ = x.shape
    h = x @ w_in
    q_a, k_a, v_a, q_i, k_i, w_i, q_b, kv_b, g_b = split_cols(h, EVEN_SPLITS)
    q_a = apply_rope(q_a.reshape(bsz, s_len, A_HEADS, HEAD_DIM), cos, sin)
    k_a = apply_rope(k_a, cos, sin)
    q_i = apply_rope(q_i.reshape(bsz, s_len, IDX_HEADS, IDX_DIM), cos, sin)
    k_i = apply_rope(k_i, cos, sin)
    o_a = dsa_attention(q_a, k_a, v_a, q_i, k_i, w_i, pos)
    k_c, v_c, k_s, v_s, k_w, v_w = jnp.split(kv_b, 6, axis=-1)
    q_b = apply_rope(q_b.reshape(bsz, s_len, B_HEADS, HEAD_DIM), cos, sin)
    o_b = nsa_attention(q_b, k_c, v_c, apply_rope(k_s, cos, sin), v_s, apply_rope(k_w, cos, sin), v_w,
                        (g_b + b_gate).reshape(bsz, s_len, B_HEADS, 3),
                        pos_k, w1_k, w2_k, pos_v, w1_v, w2_v, pos, cos, sin)
    o = jnp.concatenate([o_a.reshape(bsz, s_len, MIX_A), o_b.reshape(bsz, s_len, MIX_B)], axis=-1)
    return o @ w_o


def odd_mixer(x, w_in, b_f, w_o, pos):
    bsz, s_len, _ = x.shape
    q, k, v, f = split_cols(x @ w_in, ODD_SPLITS)
    shp = (bsz, s_len, C_HEADS, HEAD_DIM)
    log_f = jax.nn.log_sigmoid((f + b_f).astype(jnp.float32))
    o = fox_attention(q.reshape(shp), k.reshape(shp), v.reshape(shp), log_f, pos)
    return o.reshape(bsz, s_len, MIX_ODD) @ w_o


def grouped_experts(xf, eidx, gates, w1, w3, w2):
    n_tok, d = xf.shape
    tk = n_tok * TOP_K
    n_blocks = (tk + N_EXPERTS * (EXPERT_BLOCK - 1)) // EXPERT_BLOCK + 1
    flat_e = eidx.reshape(tk)
    flat_tok = jnp.arange(tk, dtype=jnp.int32) // TOP_K
    flat_g = gates.reshape(tk)
    order = jnp.argsort(flat_e)
    se = flat_e[order]
    counts = jnp.bincount(flat_e, length=N_EXPERTS)
    padded = (counts + EXPERT_BLOCK - 1) // EXPERT_BLOCK * EXPERT_BLOCK
    start_sorted = jnp.cumsum(counts) - counts
    pad_end = jnp.cumsum(padded)
    start_pad = pad_end - padded
    dest = start_pad[se] + (jnp.arange(tk, dtype=jnp.int32) - start_sorted[se])
    n_slots = n_blocks * EXPERT_BLOCK
    slot_tok = jnp.full((n_slots,), n_tok, jnp.int32).at[dest].set(flat_tok[order])
    slot_gate = jnp.zeros((n_slots,), jnp.float32).at[dest].set(flat_g[order])
    blk_e = jnp.minimum(jnp.searchsorted(pad_end, jnp.arange(n_blocks, dtype=jnp.int32) * EXPERT_BLOCK,
                                         side='right'), N_EXPERTS - 1)
    xpad = jnp.concatenate([xf, jnp.zeros((1, d), xf.dtype)], axis=0)

    def step(out, args):
        tok, g, e = args
        xe = xpad[tok]
        h = jax.nn.silu(xe @ w1[e]) * (xe @ w3[e])
        y = (h @ w2[e]) * g[:, None].astype(xe.dtype)
        return out.at[tok].add(y), None

    out, _ = lax.scan(step, jnp.zeros((n_tok + 1, d), xf.dtype),
                      (slot_tok.reshape(n_blocks, EXPERT_BLOCK), slot_gate.reshape(n_blocks, EXPERT_BLOCK), blk_e))
    return out[:n_tok]


def moe_ffn(x, router_w, router_bias, w1, w3, w2, sw1, sw3, sw2):
    bsz, s_len, d = x.shape
    n_tok = bsz * s_len
    xf = x.reshape(n_tok, d)
    shared = (jax.nn.silu(xf @ sw1) * (xf @ sw3)) @ sw2
    s = jax.nn.sigmoid(jnp.matmul(xf, router_w, preferred_element_type=jnp.float32))
    sb = s + router_bias.astype(jnp.float32)
    gscore = lax.top_k(sb.reshape(n_tok, N_GROUPS, N_EXPERTS // N_GROUPS), 2)[0].sum(-1)
    _, gsel = lax.top_k(gscore, TOPK_GROUPS)
    gmask = jnp.any(gsel[..., None] == jnp.arange(N_GROUPS), axis=-2)
    emask = jnp.repeat(gmask, N_EXPERTS // N_GROUPS, axis=-1)
    _, eidx = lax.top_k(jnp.where(emask, sb, NEG), TOP_K)
    sel = jnp.take_along_axis(s, eidx, axis=-1)
    gates = sel / jnp.sum(sel, axis=-1, keepdims=True) * ROUTED_SCALE
    routed = grouped_experts(xf, eidx, gates, w1, w3, w2)
    return (shared + routed).reshape(bsz, s_len, d)


def setup_inputs(seed: int = 0) -> dict:
    key = jax.random.key(seed)
    ks = jax.random.split(key, 32)

    def nrm(k, shape, scale):
        return jax.random.normal(k, shape, jnp.float32) * scale

    d, f, e = D_MODEL, D_EXPERT, N_EXPERTS
    cl = CMP_LEN * HEAD_DIM
    return {
        'x': nrm(ks[0], (BATCH, SEQ, d), 1.0),
        'p': nrm(ks[1], (DEPTH, BATCH, SEQ, PLE_DIM), 1.0),
        'ev_w_in': nrm(ks[2], (N_EVEN, d, EVEN_COLS), d ** -0.5),
        'ev_b_gate': nrm(ks[3], (N_EVEN, 3 * B_HEADS), 0.1),
        'ev_cmp_pos_k': nrm(ks[4], (N_EVEN, CMP_LEN, HEAD_DIM), 0.1),
        'ev_cmp_w1_k': nrm(ks[5], (N_EVEN, cl, HEAD_DIM), cl ** -0.5),
        'ev_cmp_w2_k': nrm(ks[6], (N_EVEN, HEAD_DIM, HEAD_DIM), HEAD_DIM ** -0.5),
        'ev_cmp_pos_v': nrm(ks[7], (N_EVEN, CMP_LEN, HEAD_DIM), 0.1),
        'ev_cmp_w1_v': nrm(ks[8], (N_EVEN, cl, HEAD_DIM), cl ** -0.5),
        'ev_cmp_w2_v': nrm(ks[9], (N_EVEN, HEAD_DIM, HEAD_DIM), HEAD_DIM ** -0.5),
        'ev_w_o': nrm(ks[10], (N_EVEN, MIX_EVEN, d), BETA * MIX_EVEN ** -0.5),
        'od_w_in': nrm(ks[11], (N_ODD, d, ODD_COLS), d ** -0.5),
        'od_b_f': FORGET_BIAS_INIT + nrm(ks[12], (N_ODD, C_HEADS), 0.1),
        'od_w_o': nrm(ks[13], (N_ODD, MIX_ODD, d), BETA * MIX_ODD ** -0.5),
        'ln1_g': 1.0 + nrm(ks[14], (DEPTH, d), 0.05),
        'ln1_b': nrm(ks[15], (DEPTH, d), 0.02),
        'ln2_g': 1.0 + nrm(ks[16], (DEPTH, d), 0.05),
        'ln2_b': nrm(ks[17], (DEPTH, d), 0.02),
        'router_w': nrm(ks[18], (DEPTH, d, e), d ** -0.5),
        'router_bias': nrm(ks[19], (DEPTH, e), 0.01),
        'exp_w1': nrm(ks[20], (DEPTH, e, d, f), d ** -0.5),
        'exp_w3': nrm(ks[21], (DEPTH, e, d, f), d ** -0.5),
        'exp_w2': nrm(ks[22], (DEPTH, e, f, d), BETA * f ** -0.5),
        'sh_w1': nrm(ks[23], (DEPTH, d, f), d ** -0.5),
        'sh_w3': nrm(ks[24], (DEPTH, d, f), d ** -0.5),
        'sh_w2': nrm(ks[25], (DEPTH, f, d), BETA * f ** -0.5),
        'ple_w_gate': nrm(ks[26], (DEPTH, d, d), d ** -0.5),
        'ple_w_proj': nrm(ks[27], (DEPTH, PLE_DIM, d), PLE_DIM ** -0.5),
        'ple_norm_g': 1.0 + nrm(ks[28], (DEPTH, d), 0.05),
    }


def reference(x, p, ev_w_in, ev_b_gate, ev_cmp_pos_k, ev_cmp_w1_k, ev_cmp_w2_k,
              ev_cmp_pos_v, ev_cmp_w1_v, ev_cmp_w2_v, ev_w_o, od_w_in, od_b_f, od_w_o,
              ln1_g, ln1_b, ln2_g, ln2_b, router_w, router_bias, exp_w1, exp_w3, exp_w2,
              sh_w1, sh_w3, sh_w2, ple_w_gate, ple_w_proj, ple_norm_g):
    s_len = x.shape[1]
    pos = jnp.arange(s_len, dtype=jnp.int32)
    cos, sin = rope_tables(pos)
    for i in range(DEPTH):
        j = i // 2
        if i % 2 == 0:
            h = even_mixer(x, ev_w_in[j], ev_b_gate[j], ev_cmp_pos_k[j], ev_cmp_w1_k[j], ev_cmp_w2_k[j],
                           ev_cmp_pos_v[j], ev_cmp_w1_v[j], ev_cmp_w2_v[j], ev_w_o[j], pos, cos, sin)
        else:
            h = odd_mixer(x, od_w_in[j], od_b_f[j], od_w_o[j], pos)
        x = layer_norm(ALPHA * x + h, ln1_g[i], ln1_b[i])
        m = moe_ffn(x, router_w[i], router_bias[i], exp_w1[i], exp_w3[i], exp_w2[i],
                    sh_w1[i], sh_w3[i], sh_w2[i])
        x = layer_norm(ALPHA * x + m, ln2_g[i], ln2_b[i])
        e_i = rms_norm(p[i] @ ple_w_proj[i], ple_norm_g[i])
        x = x + jax.nn.sigmoid(x @ ple_w_gate[i]) * e_i
    return x
```

```python
import functools

import numpy as np
import jax
import jax.numpy as jnp
from jax import lax
from jax.experimental import pallas as pl
from jax.experimental.pallas import tpu as pltpu

HEAD_DIM = 64
QBLOCK = 128
ROPE_THETA = 10000.0
LN_EPS = 1e-5
RMS_EPS = 1e-6
NEG = -1e30
BIG = 1e30

A_HEADS = 8
IDX_HEADS = 8
IDX_DIM = 64
DSA_TOPK_MAX = 256

B_HEADS = 8
CMP_LEN = 32
CMP_STRIDE = 16
SEL_LEN = 32
SEL_BLOCKS_MAX = 8
WINDOW = 512

C_HEADS = 16

N_EXPERTS = 64
TOP_K = 8
N_GROUPS = 8
TOPK_GROUPS = 4
ROUTED_SCALE = 2.5
EXPERT_BLOCK = 256

LANES = 128
SUBLANES = 8
VMEM_LIMIT = 48 * 1024 * 1024

MXU_DTYPE = jnp.bfloat16
INT_MIN = -(2 ** 31)
IDX_ALL = 2 ** 30


def _cparams(*sem):
    return pltpu.CompilerParams(dimension_semantics=sem, vmem_limit_bytes=VMEM_LIMIT)


def _dot(a, b):
    return jnp.dot(a, b, preferred_element_type=jnp.float32)


def _dot_nt(a, b):
    return lax.dot_general(a, b, (((1,), (1,)), ((), ())), preferred_element_type=jnp.float32)


def _silu(x):
    return x * (1.0 / (1.0 + jnp.exp(-x)))


def _sigmoid(x):
    return 1.0 / (1.0 + jnp.exp(-x))


def _mm_kernel(x_ref, w_ref, o_ref):
    o_ref[...] = _dot(x_ref[...].astype(MXU_DTYPE), w_ref[...])


def _pick_tm(t, cap):
    tm = min(cap, t)
    while t % tm:
        tm //= 2
    return tm


def _matmul(x, w, *, tm_cap=512, name="proj"):
    t, k = x.shape
    n = w.shape[1]
    n_pad = -(-n // LANES) * LANES
    w = jnp.pad(w, ((0, 0), (0, n_pad - n))).astype(MXU_DTYPE)
    tm = _pick_tm(t, tm_cap)
    out = pl.pallas_call(
        _mm_kernel,
        out_shape=jax.ShapeDtypeStruct((t, n_pad), jnp.float32),
        grid=(t // tm,),
        in_specs=[pl.BlockSpec((tm, k), lambda i: (i, 0)),
                  pl.BlockSpec((k, n_pad), lambda i: (0, 0))],
        out_specs=pl.BlockSpec((tm, n_pad), lambda i: (i, 0)),
        compiler_params=_cparams("parallel"),
        name=name,
    )(x, w)
    return out


def _online_update(s, keep, vt, m_ref, l_ref, acc_ref):
    if keep is not None:
        s = s + (keep - 1.0) * (-NEG)
    m_old = m_ref[...]
    m_new = jnp.maximum(m_old, jnp.max(s, axis=0, keepdims=True))
    p = jnp.exp(s - m_new)
    if keep is not None:
        p = p * keep
    alpha = jnp.exp(m_old - m_new)
    l_ref[...] = alpha * l_ref[...] + jnp.sum(p, axis=0, keepdims=True)
    acc_ref[...] = alpha * acc_ref[...] + _dot(vt, p.astype(vt.dtype))
    m_ref[...] = m_new


def _keep(pred, reps=1):
    k = jnp.where(pred, 1.0, 0.0)
    return k if reps == 1 else jnp.concatenate([k] * reps, axis=1)


def _init_state(m_ref, l_ref, acc_ref):
    m_ref[...] = jnp.full(m_ref.shape, NEG, jnp.float32)
    l_ref[...] = jnp.zeros(l_ref.shape, jnp.float32)
    acc_ref[...] = jnp.zeros(acc_ref.shape, jnp.float32)


def _normalized(l_ref, acc_ref):
    l = l_ref[...]
    return jnp.where(l > 0.0, acc_ref[...] / jnp.where(l > 0.0, l, 1.0), 0.0)


def _store_heads(o_t, o_ref, n_heads, tq):
    for j in range(n_heads // 2):
        pair = jnp.concatenate([o_t[:, (2 * j) * tq:(2 * j + 1) * tq],
                                o_t[:, (2 * j + 1) * tq:(2 * j + 2) * tq]], axis=0)
        o_ref[0, :, 2 * j * HEAD_DIM:(2 * j + 2) * HEAD_DIM] = pair.T.astype(o_ref.dtype)


def _dsa_kernel(qi_ref, w_ref, ki_ref, qa_ref, ka_ref, vat_ref, o_ref,
                key_ref, m_ref, l_ref, acc_ref, *, k_sel, idx_scale, scale):
    i = pl.program_id(1)
    tq = QBLOCK
    n_tiles = i + 1
    q0 = i * tq
    t_row = q0 + lax.broadcasted_iota(jnp.int32, (1, tq), 1)
    kk = lax.broadcasted_iota(jnp.int32, (tq, tq), 0)

    qi = qi_ref[0].reshape(IDX_HEADS * tq, IDX_DIM)
    w_row = w_ref[0, 0]

    def score_tile(kt, carry):
        k0 = pl.multiple_of(kt * tq, tq)
        logits = _dot_nt(ki_ref[0, pl.ds(k0, tq), :], qi)
        z = jnp.maximum(logits, 0.0) * w_row
        sc = z[:, 0:tq]
        for h in range(1, IDX_HEADS):
            sc = sc + z[:, h * tq:(h + 1) * tq]
        sc = sc * idx_scale
        sc = jnp.where(k0 + kk <= t_row, sc, NEG)
        bits = lax.bitcast_convert_type(sc, jnp.int32)
        key_ref[pl.ds(k0, tq), :] = bits ^ ((bits >> 31) & 0x7FFFFFFF)
        return carry

    lax.fori_loop(0, n_tiles, score_tile, 0)

    def count(pred_fn):
        def body(kt, acc):
            k0 = pl.multiple_of(kt * tq, tq)
            c = pred_fn(key_ref[pl.ds(k0, tq), :], k0 + kk).astype(jnp.int32)
            return acc + jnp.sum(c.reshape(tq // SUBLANES, SUBLANES, tq), axis=0)
        acc = lax.fori_loop(0, n_tiles, body, jnp.zeros((SUBLANES, tq), jnp.int32))
        return jnp.sum(acc, axis=0, keepdims=True)

    def search(_):
        c0 = count(lambda key, idx: key >= 0)
        thr = jnp.where(c0 >= k_sel, 0, INT_MIN).astype(jnp.int32)

        def bit_step(b, thr):
            trial = thr | (jnp.int32(1) << (30 - b))
            c = count(lambda key, idx: key >= trial)
            return jnp.where(c >= k_sel, trial, thr)

        thr = lax.fori_loop(0, 31, bit_step, thr)
        c_ge = count(lambda key, idx: key >= thr)

        def tie_search(_):
            c_gt = count(lambda key, idx: key > thr)
            need = k_sel - c_gt

            def idx_step(b, u):
                trial = u | (jnp.int32(1) << (20 - b))
                c = count(lambda key, idx: (key == thr) & (idx < trial))
                return jnp.where(c < need, trial, u)

            return lax.fori_loop(0, 21, idx_step, jnp.zeros((1, tq), jnp.int32))

        cut = lax.cond(jnp.max(c_ge) > k_sel, tie_search,
                       lambda _: jnp.full((1, tq), IDX_ALL, jnp.int32), 0)
        return thr, cut

    thr, cut = lax.cond(n_tiles * tq > k_sel, search,
                        lambda _: (jnp.full((1, tq), INT_MIN, jnp.int32),
                                   jnp.full((1, tq), IDX_ALL, jnp.int32)), 0)

    qa = qa_ref[0].reshape(A_HEADS * tq, HEAD_DIM)
    _init_state(m_ref, l_ref, acc_ref)

    def attn_tile(kt, carry):
        k0 = pl.multiple_of(kt * tq, tq)
        key = key_ref[pl.ds(k0, tq), :]
        idx = k0 + kk
        sel = ((key > thr) | ((key == thr) & (idx <= cut))) & (idx <= t_row)
        s = _dot_nt(ka_ref[0, pl.ds(k0, tq), :], qa) * scale
        _online_update(s, _keep(sel, A_HEADS), vat_ref[0, kt], m_ref, l_ref, acc_ref)
        return carry

    lax.fori_loop(0, n_tiles, attn_tile, 0)
    _store_heads(_normalized(l_ref, acc_ref), o_ref, A_HEADS, tq)


def _heads_major(x, n_heads):
    b, s, _ = x.shape
    return jnp.transpose(x.reshape(b, s, n_heads, HEAD_DIM), (0, 2, 1, 3)).astype(MXU_DTYPE)


def _value_tiles_t(v, tk):
    b, s, d = v.shape
    return jnp.transpose(v.reshape(b, s // tk, tk, d), (0, 1, 3, 2)).astype(MXU_DTYPE)


def _per_query_rows(x, tq):
    b, s, h = x.shape
    return jnp.transpose(x.reshape(b, s // tq, tq, h), (0, 1, 3, 2)).reshape(b, s // tq, 1, h * tq)


def _dsa_attention(q_a, k_a, v_a, q_i, k_i, w_i):
    b, s, _ = q_a.shape
    tq = QBLOCK
    nq = s // tq
    k_sel = min(DSA_TOPK_MAX, s // 4)
    kern = functools.partial(_dsa_kernel, k_sel=k_sel,
                             idx_scale=float((IDX_HEADS * IDX_DIM) ** -0.5), scale=float(HEAD_DIM ** -0.5))
    hq = A_HEADS * tq
    return pl.pallas_call(
        kern,
        out_shape=jax.ShapeDtypeStruct((b, s, A_HEADS * HEAD_DIM), MXU_DTYPE),
        grid=(b, nq),
        in_specs=[
            pl.BlockSpec((1, IDX_HEADS, tq, IDX_DIM), lambda bi, i: (bi, 0, i, 0)),
            pl.BlockSpec((1, 1, 1, hq), lambda bi, i: (bi, i, 0, 0)),
            pl.BlockSpec((1, s, IDX_DIM), lambda bi, i: (bi, 0, 0)),
            pl.BlockSpec((1, A_HEADS, tq, HEAD_DIM), lambda bi, i: (bi, 0, i, 0)),
            pl.BlockSpec((1, s, HEAD_DIM), lambda bi, i: (bi, 0, 0)),
            pl.BlockSpec((1, nq, HEAD_DIM, tq), lambda bi, i: (bi, 0, 0, 0)),
        ],
        out_specs=pl.BlockSpec((1, tq, A_HEADS * HEAD_DIM), lambda bi, i: (bi, i, 0)),
        scratch_shapes=[pltpu.VMEM((s, tq), jnp.int32),
                        pltpu.VMEM((1, hq), jnp.float32),
                        pltpu.VMEM((1, hq), jnp.float32),
                        pltpu.VMEM((HEAD_DIM, hq), jnp.float32)],
        compiler_params=_cparams("parallel", "arbitrary"),
        name="dsa_attention",
    )(_heads_major(q_i, IDX_HEADS), _per_query_rows(w_i.astype(jnp.float32), tq),
      k_i.astype(MXU_DTYPE), _heads_major(q_a, A_HEADS), k_a.astype(MXU_DTYPE), _value_tiles_t(v_a, tq))


def _cmp_kernel(ck_ref, pe_ref, w1_ref, w2_ref, o_ref):
    nc = ck_ref.shape[1]
    half = ck_ref.shape[2]
    ck = ck_ref[0]
    a = _dot((ck + pe_ref[0:1, :]).astype(MXU_DTYPE), w1_ref[0:half, :])
    bm = _dot((ck + pe_ref[1:2, :]).astype(MXU_DTYPE), w1_ref[half:2 * half, :])
    h = a + pltpu.roll(bm, nc - 1, 0)
    o_ref[0] = _dot(_silu(h).astype(MXU_DTYPE), w2_ref[...])


def _compress(kv, pe, w1, w2):
    b, s, d = kv.shape
    nc = s // CMP_STRIDE
    half = CMP_STRIDE * d
    ck = kv.reshape(b, nc, half)
    pe2 = pe.reshape(2, half)
    return pl.pallas_call(
        _cmp_kernel,
        out_shape=jax.ShapeDtypeStruct((b, nc, d), jnp.float32),
        grid=(b,),
        in_specs=[pl.BlockSpec((1, nc, half), lambda bi: (bi, 0, 0)),
                  pl.BlockSpec((2, half), lambda bi: (0, 0)),
                  pl.BlockSpec((2 * half, d), lambda bi: (0, 0)),
                  pl.BlockSpec((d, d), lambda bi: (0, 0))],
        out_specs=pl.BlockSpec((1, nc, d), lambda bi: (bi, 0, 0)),
        compiler_params=_cparams("parallel"),
        name="nsa_compress",
    )(ck, pe2, w1.astype(MXU_DTYPE), w2.astype(MXU_DTYPE))


def _split3(x):
    def top(v):
        return lax.bitcast_convert_type(lax.bitcast_convert_type(v, jnp.int32) & jnp.int32(-65536), jnp.float32)
    hi = top(x)
    r1 = x - hi
    mid = top(r1)
    lo = r1 - mid
    return hi.astype(MXU_DTYPE), mid.astype(MXU_DTYPE), lo.astype(MXU_DTYPE)


def _nsa_kernel(q_ref, g_ref, kc_ref, vct_ref, ovt_ref, exp_ref, ks_ref, vst_ref, kw_ref, vwt_ref, o_ref,
                tok_ref, m_ref, l_ref, acc_ref, out_ref, *, n_sel, n_cmp, scale):
    i = pl.program_id(1)
    tq = QBLOCK
    nh = B_HEADS
    hq = nh * tq
    q0 = i * tq
    n_tiles = i + 1
    t_row = q0 + lax.broadcasted_iota(jnp.int32, (1, tq), 1)
    kk = lax.broadcasted_iota(jnp.int32, (tq, tq), 0)
    q = q_ref[0].reshape(hq, HEAD_DIM)
    gates = g_ref[0, 0]

    nc = kc_ref.shape[1]
    c_id = lax.broadcasted_iota(jnp.int32, (nc, tq), 0)
    valid_c = (c_id * CMP_STRIDE + (CMP_LEN - 1) <= t_row) & (c_id < n_cmp)
    keep_c = _keep(valid_c, nh)
    s_c = _dot_nt(kc_ref[0], q) * scale + (keep_c - 1.0) * (-NEG)
    e_c = jnp.exp(s_c - jnp.max(s_c, axis=0, keepdims=True)) * keep_c
    den = jnp.sum(e_c, axis=0, keepdims=True)
    p_c = e_c / jnp.where(den > 0.0, den, 1.0)
    out_ref[...] = gates[0:1, :] * _dot(vct_ref[0], p_c.astype(MXU_DTYPE))

    p_sum = p_c[:, 0:tq]
    for h in range(1, nh):
        p_sum = p_sum + p_c[:, h * tq:(h + 1) * tq]
    ovt = ovt_ref[...]
    pieces = _split3(p_sum)
    imp = _dot(ovt, pieces[0]) + _dot(ovt, pieces[1]) + _dot(ovt, pieces[2])
    n_sb = ovt.shape[0]
    j_id = lax.broadcasted_iota(jnp.int32, (n_sb, tq), 0)
    cur = t_row >> (SEL_LEN.bit_length() - 1)
    forced = (j_id == 0) | (j_id == cur) | (j_id == cur - 1)
    future = j_id * SEL_LEN > t_row
    imp = jnp.where(forced, BIG, jnp.where(future, NEG, imp))
    chosen = jnp.zeros((n_sb, tq), jnp.bool_)
    for _ in range(n_sel):
        cand = jnp.where(chosen, -jnp.inf, imp)
        best = jnp.max(cand, axis=0, keepdims=True)
        first = jnp.min(jnp.where(cand == best, j_id, n_sb), axis=0, keepdims=True)
        chosen = chosen | (j_id == first)
    tok_ref[...] = _dot(exp_ref[...], jnp.where(chosen, 1.0, 0.0).astype(MXU_DTYPE))

    _init_state(m_ref, l_ref, acc_ref)

    def sel_tile(kt, carry):
        k0 = pl.multiple_of(kt * tq, tq)
        sel = (tok_ref[pl.ds(k0, tq), :] > 0.5) & (k0 + kk <= t_row)
        s = _dot_nt(ks_ref[0, pl.ds(k0, tq), :], q) * scale
        _online_update(s, _keep(sel, nh), vst_ref[0, kt], m_ref, l_ref, acc_ref)
        return carry

    lax.fori_loop(0, n_tiles, sel_tile, 0)
    out_ref[...] += gates[1:2, :] * _normalized(l_ref, acc_ref)

    _init_state(m_ref, l_ref, acc_ref)

    def win_tile(kt, carry):
        k0 = pl.multiple_of(kt * tq, tq)
        kpos = k0 + kk
        ok = (kpos <= t_row) & (kpos > t_row - WINDOW)
        s = _dot_nt(kw_ref[0, pl.ds(k0, tq), :], q) * scale
        _online_update(s, _keep(ok, nh), vwt_ref[0, kt], m_ref, l_ref, acc_ref)
        return carry

    lax.fori_loop(jnp.maximum(i - WINDOW // tq, 0), n_tiles, win_tile, 0)
    out_ref[...] += gates[2:3, :] * _normalized(l_ref, acc_ref)
    _store_heads(out_ref[...], o_ref, nh, tq)


def _nsa_attention(q_b, k_c, v_c, k_s, v_s, k_w, v_w, gate):
    b, s, _ = q_b.shape
    tq = QBLOCK
    nq = s // tq
    nh = B_HEADS
    hq = nh * tq
    nc = k_c.shape[1]
    n_cmp = (s - CMP_LEN) // CMP_STRIDE + 1
    n_sb = s // SEL_LEN
    n_sel = min(SEL_BLOCKS_MAX, n_sb)
    c_start = np.arange(nc) * CMP_STRIDE
    s_start = np.arange(n_sb) * SEL_LEN
    ov = np.clip(np.minimum(c_start[:, None] + CMP_LEN, s_start[None, :] + SEL_LEN)
                 - np.maximum(c_start[:, None], s_start[None, :]), 0, None) / CMP_LEN
    ov[n_cmp:] = 0.0
    ovt = jnp.asarray(ov.T, dtype=MXU_DTYPE)
    expand = jnp.asarray((np.arange(s)[:, None] // SEL_LEN == np.arange(n_sb)[None, :]), dtype=MXU_DTYPE)
    g = _sigmoid(gate.astype(jnp.float32))
    g = jnp.transpose(g.reshape(b, nq, tq, nh, 3), (0, 1, 4, 3, 2)).reshape(b, nq, 3, hq)
    kern = functools.partial(_nsa_kernel, n_sel=n_sel, n_cmp=n_cmp, scale=float(HEAD_DIM ** -0.5))
    full = lambda shape: pl.BlockSpec(shape, lambda bi, i: (0,) * len(shape))
    per_b = lambda shape: pl.BlockSpec(shape, lambda bi, i: (bi,) + (0,) * (len(shape) - 1))
    return pl.pallas_call(
        kern,
        out_shape=jax.ShapeDtypeStruct((b, s, nh * HEAD_DIM), MXU_DTYPE),
        grid=(b, nq),
        in_specs=[
            pl.BlockSpec((1, nh, tq, HEAD_DIM), lambda bi, i: (bi, 0, i, 0)),
            pl.BlockSpec((1, 1, 3, hq), lambda bi, i: (bi, i, 0, 0)),
            per_b((1, nc, HEAD_DIM)),
            per_b((1, HEAD_DIM, nc)),
            full((n_sb, nc)),
            full((s, n_sb)),
            per_b((1, s, HEAD_DIM)),
            per_b((1, nq, HEAD_DIM, tq)),
            per_b((1, s, HEAD_DIM)),
            per_b((1, nq, HEAD_DIM, tq)),
        ],
        out_specs=pl.BlockSpec((1, tq, nh * HEAD_DIM), lambda bi, i: (bi, i, 0)),
        scratch_shapes=[pltpu.VMEM((s, tq), jnp.float32),
                        pltpu.VMEM((1, hq), jnp.float32),
                        pltpu.VMEM((1, hq), jnp.float32),
                        pltpu.VMEM((HEAD_DIM, hq), jnp.float32),
                        pltpu.VMEM((HEAD_DIM, hq), jnp.float32)],
        compiler_params=_cparams("parallel", "arbitrary"),
        name="nsa_attention",
    )(_heads_major(q_b, nh), g, k_c.astype(MXU_DTYPE),
      jnp.transpose(v_c, (0, 2, 1)).astype(MXU_DTYPE), ovt, expand,
      k_s.astype(MXU_DTYPE), _value_tiles_t(v_s, tq), k_w.astype(MXU_DTYPE), _value_tiles_t(v_w, tq))


FOX_TQ = 256
FOX_KPAD = 128


def _fox_kernel(q_ref, k_ref, vt_ref, o_ref, m_ref, l_ref, acc_ref, out_ref):
    i = pl.program_id(2)
    tq = FOX_TQ
    kk = lax.broadcasted_iota(jnp.int32, (tq, tq), 0)
    qq = lax.broadcasted_iota(jnp.int32, (tq, tq), 1)
    for hh in range(2):
        q = q_ref[0, hh]
        _init_state(m_ref, l_ref, acc_ref)

        def full_tile(kt, carry):
            k0 = pl.multiple_of(kt * tq, tq)
            s = _dot_nt(k_ref[0, hh, pl.ds(k0, tq), :], q)
            _online_update(s, None, vt_ref[0, hh, kt], m_ref, l_ref, acc_ref)
            return carry

        lax.fori_loop(0, i, full_tile, 0)
        k0 = pl.multiple_of(i * tq, tq)
        s = _dot_nt(k_ref[0, hh, pl.ds(k0, tq), :], q)
        _online_update(s, _keep(kk <= qq), vt_ref[0, hh, i], m_ref, l_ref, acc_ref)
        out_ref[hh * HEAD_DIM:(hh + 1) * HEAD_DIM, :] = _normalized(l_ref, acc_ref)
    o_ref[0] = out_ref[...].T.astype(o_ref.dtype)


def _fox_attention(q, k, v, log_f):
    b, s, _ = q.shape
    nh = C_HEADS
    tq = min(FOX_TQ, s)
    assert tq == FOX_TQ
    d_cum = jnp.cumsum(log_f, axis=1)
    d3 = jnp.stack(_split3(d_cum), axis=-1)
    ones = jnp.ones((b, s, nh, 3), MXU_DTYPE)
    pad = jnp.zeros((b, s, nh, FOX_KPAD - HEAD_DIM - 6), MXU_DTYPE)
    qh = (q * (HEAD_DIM ** -0.5)).reshape(b, s, nh, HEAD_DIM).astype(MXU_DTYPE)
    kh = k.reshape(b, s, nh, HEAD_DIM).astype(MXU_DTYPE)
    q_aug = jnp.transpose(jnp.concatenate([qh, ones, d3, pad], axis=-1), (0, 2, 1, 3))
    k_aug = jnp.transpose(jnp.concatenate([kh, -d3, ones, pad], axis=-1), (0, 2, 1, 3))
    vt = jnp.transpose(v.reshape(b, s // tq, tq, nh, HEAD_DIM), (0, 3, 1, 4, 2)).astype(MXU_DTYPE)
    return pl.pallas_call(
        _fox_kernel,
        out_shape=jax.ShapeDtypeStruct((b, s, nh * HEAD_DIM), MXU_DTYPE),
        grid=(b, nh // 2, s // tq),
        in_specs=[
            pl.BlockSpec((1, 2, tq, FOX_KPAD), lambda bi, j, i: (bi, j, i, 0)),
            pl.BlockSpec((1, 2, s, FOX_KPAD), lambda bi, j, i: (bi, j, 0, 0)),
            pl.BlockSpec((1, 2, s // tq, HEAD_DIM, tq), lambda bi, j, i: (bi, j, 0, 0, 0)),
        ],
        out_specs=pl.BlockSpec((1, tq, 2 * HEAD_DIM), lambda bi, j, i: (bi, i, j)),
        scratch_shapes=[pltpu.VMEM((1, tq), jnp.float32),
                        pltpu.VMEM((1, tq), jnp.float32),
                        pltpu.VMEM((HEAD_DIM, tq), jnp.float32),
                        pltpu.VMEM((2 * HEAD_DIM, tq), jnp.float32)],
        compiler_params=_cparams("parallel", "parallel", "arbitrary"),
        name="fox_attention",
    )(q_aug, k_aug, vt)


def _layer_norm(y, g, b):
    mu = jnp.mean(y, axis=-1, keepdims=True)
    d = y - mu
    var = jnp.mean(d * d, axis=-1, keepdims=True)
    return d * lax.rsqrt(var + LN_EPS) * g + b


def _mid_kernel(o_ref, x_ref, wo_ref, g_ref, b_ref, sw1_ref, sw3_ref, sw2_ref, rw_ref,
                x1_ref, sh_ref, lg_ref, *, alpha):
    y = alpha * x_ref[...] + _dot(o_ref[...], wo_ref[...])
    x1 = _layer_norm(y, g_ref[...], b_ref[...])
    x1_ref[...] = x1
    xb = x1.astype(MXU_DTYPE)
    hid = _silu(_dot(xb, sw1_ref[...])) * _dot(xb, sw3_ref[...])
    sh_ref[...] = _dot(hid.astype(MXU_DTYPE), sw2_ref[...])
    lg_ref[...] = _dot(xb, rw_ref[...])


def _mid_block(o, x, w_o, ln_g, ln_b, sw1, sw3, sw2, router_w, alpha):
    t, d = x.shape
    f = sw1.shape[1]
    e = router_w.shape[1]
    e_pad = -(-e // LANES) * LANES
    rw = jnp.pad(router_w, ((0, 0), (0, e_pad - e))).astype(MXU_DTYPE)
    tm = _pick_tm(t, 512)
    row = lambda n: pl.BlockSpec((tm, n), lambda i: (i, 0))
    res = lambda a, c: pl.BlockSpec((a, c), lambda i: (0, 0))
    x1, sh, lg = pl.pallas_call(
        functools.partial(_mid_kernel, alpha=alpha),
        out_shape=(jax.ShapeDtypeStruct((t, d), jnp.float32),
                   jax.ShapeDtypeStruct((t, d), jnp.float32),
                   jax.ShapeDtypeStruct((t, e_pad), jnp.float32)),
        grid=(t // tm,),
        in_specs=[row(o.shape[1]), row(d), res(o.shape[1], d), res(1, d), res(1, d),
                  res(d, f), res(d, f), res(f, d), res(d, e_pad)],
        out_specs=(row(d), row(d), row(e_pad)),
        compiler_params=_cparams("parallel"),
        name="mixer_out_ln_shared_router",
    )(o, x, w_o.astype(MXU_DTYPE), ln_g.reshape(1, d), ln_b.reshape(1, d),
      sw1.astype(MXU_DTYPE), sw3.astype(MXU_DTYPE), sw2.astype(MXU_DTYPE), rw)
    return x1, sh, lg[:, :e]


def _expert_kernel(blk_e_ref, xs_ref, g_ref, w1_ref, w3_ref, w2_ref, y_ref):
    xe = xs_ref[...]
    hid = _silu(_dot(xe, w1_ref[0])) * _dot(xe, w3_ref[0])
    y_ref[...] = _dot(hid.astype(MXU_DTYPE), w2_ref[0]) * g_ref[...]


def _grouped_experts(xs, slot_gate, blk_e, w1, w3, w2):
    n_slots, d = xs.shape
    f = w1.shape[2]
    n_blocks = n_slots // EXPERT_BLOCK
    return pl.pallas_call(
        _expert_kernel,
        out_shape=jax.ShapeDtypeStruct((n_slots, d), jnp.float32),
        grid_spec=pltpu.PrefetchScalarGridSpec(
            num_scalar_prefetch=1,
            grid=(n_blocks,),
            in_specs=[pl.BlockSpec((EXPERT_BLOCK, d), lambda i, be: (i, 0)),
                      pl.BlockSpec((EXPERT_BLOCK, 1), lambda i, be: (i, 0)),
                      pl.BlockSpec((1, d, f), lambda i, be: (be[i], 0, 0)),
                      pl.BlockSpec((1, d, f), lambda i, be: (be[i], 0, 0)),
                      pl.BlockSpec((1, f, d), lambda i, be: (be[i], 0, 0))],
            out_specs=pl.BlockSpec((EXPERT_BLOCK, d), lambda i, be: (i, 0)),
        ),
        compiler_params=_cparams("arbitrary"),
        name="routed_experts",
    )(blk_e, xs, slot_gate.reshape(n_slots, 1), w1.astype(MXU_DTYPE), w3.astype(MXU_DTYPE), w2.astype(MXU_DTYPE))


def _route(logits, router_bias):
    n_tok = logits.shape[0]
    s = _sigmoid(logits)
    sb = s + router_bias.astype(jnp.float32)
    gscore = lax.top_k(sb.reshape(n_tok, N_GROUPS, N_EXPERTS // N_GROUPS), 2)[0].sum(-1)
    _, gsel = lax.top_k(gscore, TOPK_GROUPS)
    gmask = jnp.any(gsel[..., None] == jnp.arange(N_GROUPS), axis=-2)
    emask = jnp.repeat(gmask, N_EXPERTS // N_GROUPS, axis=-1)
    _, eidx = lax.top_k(jnp.where(emask, sb, NEG), TOP_K)
    sel = jnp.take_along_axis(s, eidx, axis=-1)
    gates = sel / jnp.sum(sel, axis=-1, keepdims=True) * ROUTED_SCALE
    return eidx, gates


def _moe_routed(x1, eidx, gates, w1, w3, w2):
    n_tok, d = x1.shape
    tk = n_tok * TOP_K
    n_blocks = (tk + N_EXPERTS * (EXPERT_BLOCK - 1)) // EXPERT_BLOCK + 1
    n_slots = n_blocks * EXPERT_BLOCK
    flat_e = eidx.reshape(tk)
    order = jnp.argsort(flat_e)
    se = flat_e[order]
    counts = jnp.bincount(flat_e, length=N_EXPERTS)
    padded = (counts + EXPERT_BLOCK - 1) // EXPERT_BLOCK * EXPERT_BLOCK
    start_sorted = jnp.cumsum(counts) - counts
    pad_end = jnp.cumsum(padded)
    start_pad = pad_end - padded
    dest = (start_pad[se] + (jnp.arange(tk, dtype=jnp.int32) - start_sorted[se])).astype(jnp.int32)
    slot_tok = jnp.full((n_slots,), n_tok, jnp.int32).at[dest].set((order // TOP_K).astype(jnp.int32))
    slot_gate = jnp.zeros((n_slots,), jnp.float32).at[dest].set(gates.reshape(tk)[order])
    blk_e = jnp.minimum(jnp.searchsorted(pad_end, jnp.arange(n_blocks, dtype=jnp.int32) * EXPERT_BLOCK,
                                         side='right'), N_EXPERTS - 1).astype(jnp.int32)
    pos = jnp.zeros((tk,), jnp.int32).at[order].set(dest).reshape(n_tok, TOP_K)
    xpad = jnp.concatenate([x1.astype(MXU_DTYPE), jnp.zeros((1, d), MXU_DTYPE)], axis=0)
    xs = xpad[slot_tok]
    y = _grouped_experts(xs, slot_gate, blk_e, w1, w3, w2)
    return y, pos


def _post_kernel(x1_ref, sh_ref, rt_ref, p_ref, g_ref, b_ref, wp_ref, ng_ref, wg_ref, o_ref, *, alpha):
    y = alpha * x1_ref[...] + sh_ref[...] + rt_ref[...]
    x2 = _layer_norm(y, g_ref[...], b_ref[...])
    e = _dot(p_ref[...].astype(MXU_DTYPE), wp_ref[...])
    e = e * lax.rsqrt(jnp.mean(e * e, axis=-1, keepdims=True) + RMS_EPS) * ng_ref[...]
    gate = _sigmoid(_dot(x2.astype(MXU_DTYPE), wg_ref[...]))
    o_ref[...] = x2 + gate * e


def _post_block(x1, shared, routed, p, ln_g, ln_b, w_proj, norm_g, w_gate, alpha):
    t, d = x1.shape
    pd = p.shape[1]
    tm = _pick_tm(t, 512)
    row = lambda n: pl.BlockSpec((tm, n), lambda i: (i, 0))
    res = lambda a, c: pl.BlockSpec((a, c), lambda i: (0, 0))
    return pl.pallas_call(
        functools.partial(_post_kernel, alpha=alpha),
        out_shape=jax.ShapeDtypeStruct((t, d), jnp.float32),
        grid=(t // tm,),
        in_specs=[row(d), row(d), row(d), row(pd), res(1, d), res(1, d), res(pd, d), res(1, d), res(d, d)],
        out_specs=row(d),
        compiler_params=_cparams("parallel"),
        name="moe_combine_ln_ple",
    )(x1, shared, routed, p, ln_g.reshape(1, d), ln_b.reshape(1, d), w_proj.astype(MXU_DTYPE),
      norm_g.reshape(1, d), w_gate.astype(MXU_DTYPE))


def _rope_tables(s_len):
    inv = 1.0 / (ROPE_THETA ** (np.arange(0, HEAD_DIM, 2, dtype=np.float32) / HEAD_DIM))
    ang = jnp.arange(s_len, dtype=jnp.float32)[:, None] * jnp.asarray(inv, dtype=jnp.float32)[None, :]
    return jnp.cos(ang), jnp.sin(ang)


def _rope(x, cos, sin):
    b, l, w = x.shape
    xh = x.reshape(b, l, w // HEAD_DIM, HEAD_DIM)
    half = HEAD_DIM // 2
    x1, x2 = xh[..., :half], xh[..., half:]
    c = cos[:, None, :]
    s = sin[:, None, :]
    return jnp.concatenate([x1 * c - x2 * s, x2 * c + x1 * s], axis=-1).reshape(b, l, w)


def _split_cols(h, sizes):
    out, c = [], 0
    for n in sizes:
        out.append(h[..., c:c + n])
        c += n
    return out


def _even_mixer(x, w_in, b_gate, pos_k, w1_k, w2_k, pos_v, w1_v, w2_v, cos, sin):
    b, s, d = x.shape
    mix = A_HEADS * HEAD_DIM
    splits = (mix, HEAD_DIM, HEAD_DIM, IDX_HEADS * IDX_DIM, IDX_DIM, IDX_HEADS, B_HEADS * HEAD_DIM,
              6 * HEAD_DIM, 3 * B_HEADS)
    h = _matmul(x.reshape(b * s, d), w_in, name="even_in_proj").reshape(b, s, -1)
    q_a, k_a, v_a, q_i, k_i, w_i, q_b, kv_b, g_b = _split_cols(h, splits)
    o_a = _dsa_attention(_rope(q_a, cos, sin), _rope(k_a, cos, sin), v_a,
                         _rope(q_i, cos, sin), _rope(k_i, cos, sin), w_i)
    k_c, v_c, k_s, v_s, k_w, v_w = _split_cols(kv_b, (HEAD_DIM,) * 6)
    nc = s // CMP_STRIDE
    c_last = jnp.minimum(jnp.arange(nc) * CMP_STRIDE + CMP_LEN - 1, s - 1)
    kc = _rope(_compress(k_c, pos_k, w1_k, w2_k), cos[c_last], sin[c_last])
    vc = _compress(v_c, pos_v, w1_v, w2_v)
    gate = (g_b + b_gate).reshape(b, s, B_HEADS, 3)
    o_b = _nsa_attention(_rope(q_b, cos, sin), kc, vc, _rope(k_s, cos, sin), v_s,
                         _rope(k_w, cos, sin), v_w, gate)
    return jnp.concatenate([o_a, o_b], axis=-1).reshape(b * s, 2 * mix)


def _odd_mixer(x, w_in, b_f):
    b, s, d = x.shape
    mix = C_HEADS * HEAD_DIM
    h = _matmul(x.reshape(b * s, d), w_in, name="odd_in_proj").reshape(b, s, -1)
    q, k, v, f = _split_cols(h, (mix, mix, mix, C_HEADS))
    log_f = jax.nn.log_sigmoid((f + b_f).astype(jnp.float32))
    return _fox_attention(q, k, v, log_f).reshape(b * s, mix)


def kernel(x, p, ev_w_in, ev_b_gate, ev_cmp_pos_k, ev_cmp_w1_k, ev_cmp_w2_k, ev_cmp_pos_v, ev_cmp_w1_v, ev_cmp_w2_v, ev_w_o, od_w_in, od_b_f, od_w_o, ln1_g, ln1_b, ln2_g, ln2_b, router_w, router_bias, exp_w1, exp_w3, exp_w2, sh_w1, sh_w3, sh_w2, ple_w_gate, ple_w_proj, ple_norm_g):
    b, s, d = x.shape
    depth = p.shape[0]
    alpha = float((2.0 * depth) ** 0.25)
    cos, sin = _rope_tables(s)
    xf = x.reshape(b * s, d)
    for i in range(depth):
        j = i // 2
        x3 = xf.reshape(b, s, d)
        if i % 2 == 0:
            o = _even_mixer(x3, ev_w_in[j], ev_b_gate[j], ev_cmp_pos_k[j], ev_cmp_w1_k[j], ev_cmp_w2_k[j],
                            ev_cmp_pos_v[j], ev_cmp_w1_v[j], ev_cmp_w2_v[j], cos, sin)
            w_o = ev_w_o[j]
        else:
            o = _odd_mixer(x3, od_w_in[j], od_b_f[j])
            w_o = od_w_o[j]
        x1, shared, logits = _mid_block(o, xf, w_o, ln1_g[i], ln1_b[i], sh_w1[i], sh_w3[i], sh_w2[i],
                                        router_w[i], alpha)
        eidx, gates = _route(logits, router_bias[i])
        y, pos = _moe_routed(x1, eidx, gates, exp_w1[i], exp_w3[i], exp_w2[i])
        routed = jnp.sum(y[pos], axis=1)
        xf = _post_block(x1, shared, routed, p[i].reshape(b * s, -1), ln2_g[i], ln2_b[i],
                         ple_w_proj[i], ple_norm_g[i], ple_w_gate[i], alpha)
    return xf.reshape(b, s, d)
```

```python
import functools

import numpy as np
import jax
import jax.numpy as jnp
from jax import lax
from jax.experimental import pallas as pl
from jax.experimental.pallas import tpu as pltpu

HEAD_DIM = 64
QBLOCK = 128
ROPE_THETA = 10000.0
LN_EPS = 1e-5
RMS_EPS = 1e-6
NEG = -1e30
BIG = 1e30

A_HEADS = 8
IDX_HEADS = 8
IDX_DIM = 64
DSA_TOPK_MAX = 256

B_HEADS = 8
CMP_LEN = 32
CMP_STRIDE = 16
SEL_LEN = 32
SEL_BLOCKS_MAX = 8
WINDOW = 512

C_HEADS = 16

N_EXPERTS = 64
TOP_K = 8
N_GROUPS = 8
TOPK_GROUPS = 4
ROUTED_SCALE = 2.5
EXPERT_BLOCK = 256

LANES = 128
SUBLANES = 8
VMEM_LIMIT = 48 * 1024 * 1024

MXU_DTYPE = jnp.bfloat16
KEY_TILE = 256
ATTN_SCALE = HEAD_DIM ** -0.5
assert ATTN_SCALE == 2.0 ** round(np.log2(ATTN_SCALE))
INT_MIN = -(2 ** 31)
IDX_ALL = 2 ** 30


def _cparams(*sem):
    return pltpu.CompilerParams(dimension_semantics=sem, vmem_limit_bytes=VMEM_LIMIT)


def _dot(a, b):
    return jnp.dot(a, b, preferred_element_type=jnp.float32)


def _dot_nt(a, b):
    return lax.dot_general(a, b, (((1,), (1,)), ((), ())), preferred_element_type=jnp.float32)


def _silu(x):
    return x * (1.0 / (1.0 + jnp.exp(-x)))


def _sigmoid(x):
    return 1.0 / (1.0 + jnp.exp(-x))


def _mm_kernel(x_ref, w_ref, o_ref):
    o_ref[...] = _dot(x_ref[...].astype(MXU_DTYPE), w_ref[...])


def _pick_tm(t, cap):
    tm = min(cap, t)
    while t % tm:
        tm //= 2
    return tm


def _matmul(x, w, *, tm_cap=512, name="proj"):
    t, k = x.shape
    n = w.shape[1]
    n_pad = -(-n // LANES) * LANES
    w = jnp.pad(w, ((0, 0), (0, n_pad - n))).astype(MXU_DTYPE)
    tm = _pick_tm(t, tm_cap)
    out = pl.pallas_call(
        _mm_kernel,
        out_shape=jax.ShapeDtypeStruct((t, n_pad), jnp.float32),
        grid=(t // tm,),
        in_specs=[pl.BlockSpec((tm, k), lambda i: (i, 0)),
                  pl.BlockSpec((k, n_pad), lambda i: (0, 0))],
        out_specs=pl.BlockSpec((tm, n_pad), lambda i: (i, 0)),
        compiler_params=_cparams("parallel"),
        name=name,
    )(x, w)
    return out


def _online_update(s, drop, vt, m_ref, l_ref, acc_ref, idx=Ellipsis):
    if drop is not None:
        s = s + drop
    m_old = m_ref[idx]
    m_new = jnp.maximum(m_old, jnp.max(s, axis=0, keepdims=True))
    p = jnp.exp(s - m_new)
    alpha = jnp.exp(m_old - m_new)
    l_ref[idx] = alpha * l_ref[idx] + jnp.sum(p, axis=0, keepdims=True)
    acc_ref[idx] = alpha * acc_ref[idx] + _dot(vt, p.astype(vt.dtype))
    m_ref[idx] = m_new


def _drop(pred, reps=1):
    d = jnp.where(pred, 0.0, NEG)
    return d if reps == 1 else jnp.concatenate([d] * reps, axis=1)


def _init_state(m_ref, l_ref, acc_ref):
    m_ref[...] = jnp.full(m_ref.shape, NEG, jnp.float32)
    l_ref[...] = jnp.zeros(l_ref.shape, jnp.float32)
    acc_ref[...] = jnp.zeros(acc_ref.shape, jnp.float32)


def _normalized(l_ref, acc_ref, idx=Ellipsis):
    return acc_ref[idx] / l_ref[idx]


def _store_heads(o_t, o_ref, n_heads, tq):
    for j in range(n_heads // 2):
        pair = jnp.concatenate([o_t[:, (2 * j) * tq:(2 * j + 1) * tq],
                                o_t[:, (2 * j + 1) * tq:(2 * j + 2) * tq]], axis=0)
        o_ref[0, :, 2 * j * HEAD_DIM:(2 * j + 2) * HEAD_DIM] = pair.T.astype(o_ref.dtype)


def _dsa_kernel(qi_ref, w_ref, ki_ref, qa_ref, ka_ref, vat_ref, o_ref,
                key_ref, m_ref, l_ref, acc_ref, *, k_sel, idx_scale):
    i = pl.program_id(1)
    tq = QBLOCK
    tk = KEY_TILE
    n_tiles = (i * tq + tq + tk - 1) // tk
    q0 = i * tq
    t_row = q0 + lax.broadcasted_iota(jnp.int32, (1, tq), 1)
    kk = lax.broadcasted_iota(jnp.int32, (tk, tq), 0)

    qi = qi_ref[0].reshape(IDX_HEADS * tq, IDX_DIM)
    w_row = w_ref[0, 0]

    def score_tile(kt, carry):
        k0 = pl.multiple_of(kt * tk, tk)
        logits = _dot_nt(ki_ref[0, pl.ds(k0, tk), :], qi)
        z = jnp.maximum(logits, 0.0) * w_row
        sc = z[:, 0:tq]
        for h in range(1, IDX_HEADS):
            sc = sc + z[:, h * tq:(h + 1) * tq]
        sc = sc * idx_scale
        sc = jnp.where(k0 + kk <= t_row, sc, NEG)
        bits = lax.bitcast_convert_type(sc, jnp.int32)
        key_ref[pl.ds(k0, tk), :] = bits ^ ((bits >> 31) & 0x7FFFFFFF)
        return carry

    lax.fori_loop(0, n_tiles, score_tile, 0)

    def count(pred_fn):
        def body(kt, acc):
            k0 = pl.multiple_of(kt * tk, tk)
            c = pred_fn(key_ref[pl.ds(k0, tk), :], k0 + kk).astype(jnp.int32)
            return acc + jnp.sum(c.reshape(tk // SUBLANES, SUBLANES, tq), axis=0)
        acc = lax.fori_loop(0, n_tiles, body, jnp.zeros((SUBLANES, tq), jnp.int32))
        return jnp.sum(acc, axis=0, keepdims=True)

    def search(_):
        c0 = count(lambda key, idx: key >= 0)
        thr = jnp.where(c0 >= k_sel, 0, INT_MIN).astype(jnp.int32)

        def bit_step(b, thr):
            trial = thr | (jnp.int32(1) << (30 - b))
            c = count(lambda key, idx: key >= trial)
            return jnp.where(c >= k_sel, trial, thr)

        thr = lax.fori_loop(0, 31, bit_step, thr)
        c_ge = count(lambda key, idx: key >= thr)

        def tie_search(_):
            c_gt = count(lambda key, idx: key > thr)
            need = k_sel - c_gt

            def idx_step(b, u):
                trial = u | (jnp.int32(1) << (20 - b))
                c = count(lambda key, idx: (key == thr) & (idx < trial))
                return jnp.where(c < need, trial, u)

            return lax.fori_loop(0, 21, idx_step, jnp.zeros((1, tq), jnp.int32))

        cut = lax.cond(jnp.max(c_ge) > k_sel, tie_search,
                       lambda _: jnp.full((1, tq), IDX_ALL, jnp.int32), 0)
        return thr, cut

    thr, cut = lax.cond(q0 + tq > k_sel, search,
                        lambda _: (jnp.full((1, tq), INT_MIN, jnp.int32),
                                   jnp.full((1, tq), IDX_ALL, jnp.int32)), 0)

    qa = qa_ref[0].reshape(A_HEADS * tq, HEAD_DIM)
    _init_state(m_ref, l_ref, acc_ref)

    def attn_tile(kt, carry):
        k0 = pl.multiple_of(kt * tk, tk)
        key = key_ref[pl.ds(k0, tk), :]
        idx = k0 + kk
        sel = ((key > thr) | ((key == thr) & (idx <= cut))) & (idx <= t_row)
        s = _dot_nt(ka_ref[0, pl.ds(k0, tk), :], qa)
        _online_update(s, _drop(sel, A_HEADS), vat_ref[0, kt], m_ref, l_ref, acc_ref)
        return carry

    lax.fori_loop(0, n_tiles, attn_tile, 0)
    _store_heads(_normalized(l_ref, acc_ref), o_ref, A_HEADS, tq)


def _heads_major(x, n_heads, scale=1.0):
    b, s, _ = x.shape
    return jnp.transpose((x * scale).reshape(b, s, n_heads, HEAD_DIM), (0, 2, 1, 3)).astype(MXU_DTYPE)


def _value_tiles_t(v, tk):
    b, s, d = v.shape
    return jnp.transpose(v.reshape(b, s // tk, tk, d), (0, 1, 3, 2)).astype(MXU_DTYPE)


def _per_query_rows(x, tq):
    b, s, h = x.shape
    return jnp.transpose(x.reshape(b, s // tq, tq, h), (0, 1, 3, 2)).reshape(b, s // tq, 1, h * tq)


def _dsa_attention(q_a, k_a, v_a, q_i, k_i, w_i):
    b, s, _ = q_a.shape
    tq = QBLOCK
    nq = s // tq
    k_sel = min(DSA_TOPK_MAX, s // 4)
    tk = KEY_TILE
    kern = functools.partial(_dsa_kernel, k_sel=k_sel, idx_scale=float((IDX_HEADS * IDX_DIM) ** -0.5))
    hq = A_HEADS * tq
    return pl.pallas_call(
        kern,
        out_shape=jax.ShapeDtypeStruct((b, s, A_HEADS * HEAD_DIM), MXU_DTYPE),
        grid=(b, nq),
        in_specs=[
            pl.BlockSpec((1, IDX_HEADS, tq, IDX_DIM), lambda bi, i: (bi, 0, i, 0)),
            pl.BlockSpec((1, 1, 1, hq), lambda bi, i: (bi, i, 0, 0)),
            pl.BlockSpec((1, s, IDX_DIM), lambda bi, i: (bi, 0, 0)),
            pl.BlockSpec((1, A_HEADS, tq, HEAD_DIM), lambda bi, i: (bi, 0, i, 0)),
            pl.BlockSpec((1, s, HEAD_DIM), lambda bi, i: (bi, 0, 0)),
            pl.BlockSpec((1, s // tk, HEAD_DIM, tk), lambda bi, i: (bi, 0, 0, 0)),
        ],
        out_specs=pl.BlockSpec((1, tq, A_HEADS * HEAD_DIM), lambda bi, i: (bi, i, 0)),
        scratch_shapes=[pltpu.VMEM((s, tq), jnp.int32),
                        pltpu.VMEM((1, hq), jnp.float32),
                        pltpu.VMEM((1, hq), jnp.float32),
                        pltpu.VMEM((HEAD_DIM, hq), jnp.float32)],
        compiler_params=_cparams("parallel", "arbitrary"),
        name="dsa_attention",
    )(_heads_major(q_i, IDX_HEADS), _per_query_rows(w_i.astype(jnp.float32), tq),
      k_i.astype(MXU_DTYPE), _heads_major(q_a, A_HEADS, ATTN_SCALE), k_a.astype(MXU_DTYPE),
      _value_tiles_t(v_a, tk))


def _cmp_kernel(ck_ref, pe_ref, w1_ref, w2_ref, o_ref):
    nc = ck_ref.shape[1]
    half = ck_ref.shape[2]
    ck = ck_ref[0]
    a = _dot((ck + pe_ref[0:1, :]).astype(MXU_DTYPE), w1_ref[0:half, :])
    bm = _dot((ck + pe_ref[1:2, :]).astype(MXU_DTYPE), w1_ref[half:2 * half, :])
    h = a + pltpu.roll(bm, nc - 1, 0)
    o_ref[0] = _dot(_silu(h).astype(MXU_DTYPE), w2_ref[...])


def _compress(kv, pe, w1, w2):
    b, s, d = kv.shape
    nc = s // CMP_STRIDE
    half = CMP_STRIDE * d
    ck = kv.reshape(b, nc, half)
    pe2 = pe.reshape(2, half)
    return pl.pallas_call(
        _cmp_kernel,
        out_shape=jax.ShapeDtypeStruct((b, nc, d), jnp.float32),
        grid=(b,),
        in_specs=[pl.BlockSpec((1, nc, half), lambda bi: (bi, 0, 0)),
                  pl.BlockSpec((2, half), lambda bi: (0, 0)),
                  pl.BlockSpec((2 * half, d), lambda bi: (0, 0)),
                  pl.BlockSpec((d, d), lambda bi: (0, 0))],
        out_specs=pl.BlockSpec((1, nc, d), lambda bi: (bi, 0, 0)),
        compiler_params=_cparams("parallel"),
        name="nsa_compress",
    )(ck, pe2, w1.astype(MXU_DTYPE), w2.astype(MXU_DTYPE))


def _split3(x):
    def top(v):
        return lax.bitcast_convert_type(lax.bitcast_convert_type(v, jnp.int32) & jnp.int32(-65536), jnp.float32)
    hi = top(x)
    r1 = x - hi
    mid = top(r1)
    lo = r1 - mid
    return hi.astype(MXU_DTYPE), mid.astype(MXU_DTYPE), lo.astype(MXU_DTYPE)


def _nsa_kernel(q_ref, g_ref, kc_ref, vct_ref, ovt_ref, exp_ref, ks_ref, vst_ref, kw_ref, vwt_ref, o_ref,
                tok_ref, m_ref, l_ref, acc_ref, m2_ref, l2_ref, acc2_ref, out_ref, *, n_sel, n_cmp):
    i = pl.program_id(1)
    tq = QBLOCK
    tk = KEY_TILE
    nh = B_HEADS
    hq = nh * tq
    q0 = i * tq
    n_tiles = (q0 + tq + tk - 1) // tk
    t_row = q0 + lax.broadcasted_iota(jnp.int32, (1, tq), 1)
    kk = lax.broadcasted_iota(jnp.int32, (tk, tq), 0)
    q = q_ref[0].reshape(hq, HEAD_DIM)
    gates = g_ref[0, 0]

    nc = kc_ref.shape[1]
    c_id = lax.broadcasted_iota(jnp.int32, (nc, tq), 0)
    valid_c = (c_id * CMP_STRIDE + (CMP_LEN - 1) <= t_row) & (c_id < n_cmp)
    keep_c = jnp.concatenate([jnp.where(valid_c, 1.0, 0.0)] * nh, axis=1)
    s_c = _dot_nt(kc_ref[0], q) + _drop(valid_c, nh)
    e_c = jnp.exp(s_c - jnp.max(s_c, axis=0, keepdims=True)) * keep_c
    den = jnp.sum(e_c, axis=0, keepdims=True)
    p_c = e_c / jnp.where(den > 0.0, den, 1.0)
    out_ref[...] = gates[0:1, :] * _dot(vct_ref[0], p_c.astype(MXU_DTYPE))

    p_sum = p_c[:, 0:tq]
    for h in range(1, nh):
        p_sum = p_sum + p_c[:, h * tq:(h + 1) * tq]
    ovt = ovt_ref[...]
    pieces = _split3(p_sum)
    imp = _dot(ovt, pieces[0]) + _dot(ovt, pieces[1]) + _dot(ovt, pieces[2])
    n_sb = ovt.shape[0]
    j_id = lax.broadcasted_iota(jnp.int32, (n_sb, tq), 0)
    cur = t_row >> (SEL_LEN.bit_length() - 1)
    forced = (j_id == 0) | (j_id == cur) | (j_id == cur - 1)
    future = j_id * SEL_LEN > t_row
    imp = jnp.where(forced, BIG, jnp.where(future, NEG, imp))
    chosen = jnp.zeros((n_sb, tq), jnp.bool_)
    for _ in range(n_sel):
        cand = jnp.where(chosen, -jnp.inf, imp)
        best = jnp.max(cand, axis=0, keepdims=True)
        first = jnp.min(jnp.where(cand == best, j_id, n_sb), axis=0, keepdims=True)
        chosen = chosen | (j_id == first)
    tok_ref[...] = _dot(exp_ref[...], jnp.where(chosen, 1.0, 0.0).astype(MXU_DTYPE))

    _init_state(m_ref, l_ref, acc_ref)
    _init_state(m2_ref, l2_ref, acc2_ref)
    win_lo = jnp.maximum(q0 - WINDOW, 0) // tk

    def sel_step(kt):
        k0 = pl.multiple_of(kt * tk, tk)
        sel = (tok_ref[pl.ds(k0, tk), :] > 0.5) & (k0 + kk <= t_row)
        s = _dot_nt(ks_ref[0, pl.ds(k0, tk), :], q)
        _online_update(s, _drop(sel, nh), vst_ref[0, kt], m_ref, l_ref, acc_ref)

    def win_step(kt):
        k0 = pl.multiple_of(kt * tk, tk)
        kpos = k0 + kk
        ok = (kpos <= t_row) & (kpos > t_row - WINDOW)
        s = _dot_nt(kw_ref[0, pl.ds(k0, tk), :], q)
        _online_update(s, _drop(ok, nh), vwt_ref[0, kt], m2_ref, l2_ref, acc2_ref)

    def sel_tile(kt, carry):
        sel_step(kt)
        return carry

    def both_tile(kt, carry):
        sel_step(kt)
        win_step(kt)
        return carry

    lax.fori_loop(0, win_lo, sel_tile, 0)
    lax.fori_loop(win_lo, n_tiles, both_tile, 0)
    o_t = out_ref[...] + gates[1:2, :] * _normalized(l_ref, acc_ref) + gates[2:3, :] * _normalized(l2_ref, acc2_ref)
    _store_heads(o_t, o_ref, nh, tq)


def _nsa_attention(q_b, k_c, v_c, k_s, v_s, k_w, v_w, gate):
    b, s, _ = q_b.shape
    tq = QBLOCK
    nq = s // tq
    nh = B_HEADS
    hq = nh * tq
    nc = k_c.shape[1]
    n_cmp = (s - CMP_LEN) // CMP_STRIDE + 1
    n_sb = s // SEL_LEN
    n_sel = min(SEL_BLOCKS_MAX, n_sb)
    c_start = np.arange(nc) * CMP_STRIDE
    s_start = np.arange(n_sb) * SEL_LEN
    ov = np.clip(np.minimum(c_start[:, None] + CMP_LEN, s_start[None, :] + SEL_LEN)
                 - np.maximum(c_start[:, None], s_start[None, :]), 0, None) / CMP_LEN
    ov[n_cmp:] = 0.0
    ovt = jnp.asarray(ov.T, dtype=MXU_DTYPE)
    expand = jnp.asarray((np.arange(s)[:, None] // SEL_LEN == np.arange(n_sb)[None, :]), dtype=MXU_DTYPE)
    g = _sigmoid(gate.astype(jnp.float32))
    g = jnp.transpose(g.reshape(b, nq, tq, nh, 3), (0, 1, 4, 3, 2)).reshape(b, nq, 3, hq)
    tk = KEY_TILE
    kern = functools.partial(_nsa_kernel, n_sel=n_sel, n_cmp=n_cmp)
    full = lambda shape: pl.BlockSpec(shape, lambda bi, i: (0,) * len(shape))
    per_b = lambda shape: pl.BlockSpec(shape, lambda bi, i: (bi,) + (0,) * (len(shape) - 1))
    state = [pltpu.VMEM((1, hq), jnp.float32), pltpu.VMEM((1, hq), jnp.float32),
             pltpu.VMEM((HEAD_DIM, hq), jnp.float32)]
    return pl.pallas_call(
        kern,
        out_shape=jax.ShapeDtypeStruct((b, s, nh * HEAD_DIM), MXU_DTYPE),
        grid=(b, nq),
        in_specs=[
            pl.BlockSpec((1, nh, tq, HEAD_DIM), lambda bi, i: (bi, 0, i, 0)),
            pl.BlockSpec((1, 1, 3, hq), lambda bi, i: (bi, i, 0, 0)),
            per_b((1, nc, HEAD_DIM)),
            per_b((1, HEAD_DIM, nc)),
            full((n_sb, nc)),
            full((s, n_sb)),
            per_b((1, s, HEAD_DIM)),
            per_b((1, s // tk, HEAD_DIM, tk)),
            per_b((1, s, HEAD_DIM)),
            per_b((1, s // tk, HEAD_DIM, tk)),
        ],
        out_specs=pl.BlockSpec((1, tq, nh * HEAD_DIM), lambda bi, i: (bi, i, 0)),
        scratch_shapes=[pltpu.VMEM((s, tq), jnp.float32)] + state + state
                       + [pltpu.VMEM((HEAD_DIM, hq), jnp.float32)],
        compiler_params=_cparams("parallel", "arbitrary"),
        name="nsa_attention",
    )(_heads_major(q_b, nh, ATTN_SCALE), g, k_c.astype(MXU_DTYPE),
      jnp.transpose(v_c, (0, 2, 1)).astype(MXU_DTYPE), ovt, expand,
      k_s.astype(MXU_DTYPE), _value_tiles_t(v_s, tk), k_w.astype(MXU_DTYPE), _value_tiles_t(v_w, tk))


FOX_TQ = 256
FOX_HB = 4
FOX_KPAD = 128


def _fox_kernel(q_ref, k_ref, vt_ref, o_ref, m_ref, l_ref, acc_ref):
    i = pl.program_id(2)
    tq = FOX_TQ
    _init_state(m_ref, l_ref, acc_ref)

    def step(kt, drop):
        k0 = pl.multiple_of(kt * tq, tq)
        for hh in range(FOX_HB):
            s = _dot_nt(k_ref[0, hh, pl.ds(k0, tq), :], q_ref[0, hh])
            _online_update(s, drop, vt_ref[0, hh, kt], m_ref, l_ref, acc_ref, idx=hh)

    def full_tile(kt, carry):
        step(kt, None)
        return carry

    lax.fori_loop(0, i, full_tile, 0)
    kk = lax.broadcasted_iota(jnp.int32, (tq, tq), 0)
    qq = lax.broadcasted_iota(jnp.int32, (tq, tq), 1)
    step(i, _drop(kk <= qq))
    for j in range(FOX_HB // 2):
        pair = jnp.concatenate([_normalized(l_ref, acc_ref, 2 * j), _normalized(l_ref, acc_ref, 2 * j + 1)], axis=0)
        o_ref[0, :, 2 * j * HEAD_DIM:(2 * j + 2) * HEAD_DIM] = pair.T.astype(o_ref.dtype)


def _fox_attention(q, k, v, log_f):
    b, s, _ = q.shape
    nh = C_HEADS
    tq = min(FOX_TQ, s)
    assert tq == FOX_TQ
    d_cum = jnp.cumsum(log_f, axis=1)
    d3 = jnp.stack(_split3(d_cum), axis=-1)
    ones = jnp.ones((b, s, nh, 3), MXU_DTYPE)
    pad = jnp.zeros((b, s, nh, FOX_KPAD - HEAD_DIM - 6), MXU_DTYPE)
    qh = (q * (HEAD_DIM ** -0.5)).reshape(b, s, nh, HEAD_DIM).astype(MXU_DTYPE)
    kh = k.reshape(b, s, nh, HEAD_DIM).astype(MXU_DTYPE)
    q_aug = jnp.transpose(jnp.concatenate([qh, ones, d3, pad], axis=-1), (0, 2, 1, 3))
    k_aug = jnp.transpose(jnp.concatenate([kh, -d3, ones, pad], axis=-1), (0, 2, 1, 3))
    vt = jnp.transpose(v.reshape(b, s // tq, tq, nh, HEAD_DIM), (0, 3, 1, 4, 2)).astype(MXU_DTYPE)
    return pl.pallas_call(
        _fox_kernel,
        out_shape=jax.ShapeDtypeStruct((b, s, nh * HEAD_DIM), MXU_DTYPE),
        grid=(b, nh // FOX_HB, s // tq),
        in_specs=[
            pl.BlockSpec((1, FOX_HB, tq, FOX_KPAD), lambda bi, j, i: (bi, j, i, 0)),
            pl.BlockSpec((1, FOX_HB, s, FOX_KPAD), lambda bi, j, i: (bi, j, 0, 0)),
            pl.BlockSpec((1, FOX_HB, s // tq, HEAD_DIM, tq), lambda bi, j, i: (bi, j, 0, 0, 0)),
        ],
        out_specs=pl.BlockSpec((1, tq, FOX_HB * HEAD_DIM), lambda bi, j, i: (bi, i, j)),
        scratch_shapes=[pltpu.VMEM((FOX_HB, 1, tq), jnp.float32),
                        pltpu.VMEM((FOX_HB, 1, tq), jnp.float32),
                        pltpu.VMEM((FOX_HB, HEAD_DIM, tq), jnp.float32)],
        compiler_params=_cparams("parallel", "parallel", "arbitrary"),
        name="fox_attention",
    )(q_aug, k_aug, vt)


def _layer_norm(y, g, b):
    mu = jnp.mean(y, axis=-1, keepdims=True)
    d = y - mu
    var = jnp.mean(d * d, axis=-1, keepdims=True)
    return d * lax.rsqrt(var + LN_EPS) * g + b


def _mid_kernel(o_ref, x_ref, wo_ref, g_ref, b_ref, sw1_ref, sw3_ref, sw2_ref, rw_ref,
                x1_ref, sh_ref, lg_ref, *, alpha):
    y = alpha * x_ref[...] + _dot(o_ref[...], wo_ref[...])
    x1 = _layer_norm(y, g_ref[...], b_ref[...])
    x1_ref[...] = x1
    xb = x1.astype(MXU_DTYPE)
    hid = _silu(_dot(xb, sw1_ref[...])) * _dot(xb, sw3_ref[...])
    sh_ref[...] = _dot(hid.astype(MXU_DTYPE), sw2_ref[...])
    lg_ref[...] = _dot_nt(rw_ref[...], xb)


def _mid_block(o, x, w_o, ln_g, ln_b, sw1, sw3, sw2, router_w, alpha):
    t, d = x.shape
    f = sw1.shape[1]
    e = router_w.shape[1]
    tm = _pick_tm(t, 512)
    row = lambda n: pl.BlockSpec((tm, n), lambda i: (i, 0))
    res = lambda a, c: pl.BlockSpec((a, c), lambda i: (0, 0))
    return pl.pallas_call(
        functools.partial(_mid_kernel, alpha=alpha),
        out_shape=(jax.ShapeDtypeStruct((t, d), jnp.float32),
                   jax.ShapeDtypeStruct((t, d), jnp.float32),
                   jax.ShapeDtypeStruct((e, t), jnp.float32)),
        grid=(t // tm,),
        in_specs=[row(o.shape[1]), row(d), res(o.shape[1], d), res(1, d), res(1, d),
                  res(d, f), res(d, f), res(f, d), res(e, d)],
        out_specs=(row(d), row(d), pl.BlockSpec((e, tm), lambda i: (0, i))),
        compiler_params=_cparams("parallel"),
        name="mixer_out_ln_shared_router",
    )(o, x, w_o.astype(MXU_DTYPE), ln_g.reshape(1, d), ln_b.reshape(1, d),
      sw1.astype(MXU_DTYPE), sw3.astype(MXU_DTYPE), sw2.astype(MXU_DTYPE), router_w.T.astype(MXU_DTYPE))


def _expert_kernel(blk_e_ref, xs_ref, w1_ref, w3_ref, w2_ref, y_ref):
    xe = xs_ref[...]
    hid = _silu(_dot(xe, w1_ref[0])) * _dot(xe, w3_ref[0])
    y_ref[...] = _dot(hid.astype(MXU_DTYPE), w2_ref[0])


def _grouped_experts(xs, blk_e, w1, w3, w2):
    n_slots, d = xs.shape
    f = w1.shape[2]
    n_blocks = n_slots // EXPERT_BLOCK
    return pl.pallas_call(
        _expert_kernel,
        out_shape=jax.ShapeDtypeStruct((n_slots, d), jnp.float32),
        grid_spec=pltpu.PrefetchScalarGridSpec(
            num_scalar_prefetch=1,
            grid=(n_blocks,),
            in_specs=[pl.BlockSpec((EXPERT_BLOCK, d), lambda i, be: (i, 0)),
                      pl.BlockSpec((1, d, f), lambda i, be: (be[i], 0, 0)),
                      pl.BlockSpec((1, d, f), lambda i, be: (be[i], 0, 0)),
                      pl.BlockSpec((1, f, d), lambda i, be: (be[i], 0, 0))],
            out_specs=pl.BlockSpec((EXPERT_BLOCK, d), lambda i, be: (i, 0)),
        ),
        compiler_params=_cparams("arbitrary"),
        name="routed_experts",
    )(blk_e, xs, w1.astype(MXU_DTYPE), w3.astype(MXU_DTYPE), w2.astype(MXU_DTYPE))


ROUTER_TM = 512


def _pick_rows(rows, row_id, n):
    out = jnp.zeros((n,) + rows[0].shape[1:], rows[0].dtype)
    for r in range(n):
        out = jnp.where(row_id == r, rows[r], out)
    return out


def _router_kernel(lg_ref, bias_ref, tri_ref, eidx_ref, gate_ref, rank_ref, cnt_ref, carry_ref):
    e, tm = lg_ref.shape
    per_group = e // N_GROUPS

    @pl.when(pl.program_id(0) == 0)
    def _():
        carry_ref[...] = jnp.zeros(carry_ref.shape, jnp.float32)

    s = _sigmoid(lg_ref[...])
    sb = s + bias_ref[...]
    neg_inf = -jnp.inf

    sub_id = lax.broadcasted_iota(jnp.int32, (per_group, tm), 0)
    g_rows = []
    for g in range(N_GROUPS):
        blk = sb[g * per_group:(g + 1) * per_group, :]
        m1 = jnp.max(blk, axis=0, keepdims=True)
        f1 = jnp.min(jnp.where(blk == m1, sub_id, per_group), axis=0, keepdims=True)
        m2 = jnp.max(jnp.where(sub_id == f1, neg_inf, blk), axis=0, keepdims=True)
        g_rows.append(m1 + m2)
    g_id = lax.broadcasted_iota(jnp.int32, (N_GROUPS, tm), 0)
    gscore = _pick_rows(g_rows, g_id, N_GROUPS)

    e_id = lax.broadcasted_iota(jnp.int32, (e, tm), 0)
    e_group = e_id // per_group if per_group & (per_group - 1) else e_id >> (per_group.bit_length() - 1)
    g_taken = jnp.zeros((N_GROUPS, tm), jnp.bool_)
    e_allowed = jnp.zeros((e, tm), jnp.bool_)
    for _ in range(TOPK_GROUPS):
        cand = jnp.where(g_taken, neg_inf, gscore)
        best = jnp.max(cand, axis=0, keepdims=True)
        first = jnp.min(jnp.where(cand == best, g_id, N_GROUPS), axis=0, keepdims=True)
        g_taken = g_taken | (g_id == first)
        e_allowed = e_allowed | (e_group == first)

    masked = jnp.where(e_allowed, sb, NEG)
    chosen = jnp.zeros((e, tm), jnp.bool_)
    id_rows, sel_rows = [], []
    for _ in range(TOP_K):
        cand = jnp.where(chosen, neg_inf, masked)
        best = jnp.max(cand, axis=0, keepdims=True)
        first = jnp.min(jnp.where(cand == best, e_id, e), axis=0, keepdims=True)
        hit = e_id == first
        chosen = chosen | hit
        id_rows.append(first)
        sel_rows.append(jnp.sum(jnp.where(hit, s, 0.0), axis=0, keepdims=True))
    k_id = lax.broadcasted_iota(jnp.int32, (TOP_K, tm), 0)
    eidx = _pick_rows(id_rows, k_id, TOP_K)
    sel = _pick_rows(sel_rows, k_id, TOP_K)
    eidx_ref[...] = eidx
    gate_ref[...] = sel / jnp.sum(sel, axis=0, keepdims=True) * ROUTED_SCALE

    chosen_f = jnp.where(chosen, 1.0, 0.0)
    incl = _dot(chosen_f.astype(MXU_DTYPE), tri_ref[...])
    rank_dense = carry_ref[...] + incl - chosen_f
    rank_rows = [jnp.sum(jnp.where(e_id == id_rows[r], rank_dense, 0.0), axis=0, keepdims=True)
                 for r in range(TOP_K)]
    rank_ref[...] = _pick_rows(rank_rows, k_id, TOP_K).astype(jnp.int32)
    carry_ref[...] = carry_ref[...] + jnp.sum(chosen_f, axis=1, keepdims=True)
    cnt_ref[...] = carry_ref[...]


def _route(logits_t, router_bias):
    e, t = logits_t.shape
    tm = _pick_tm(t, ROUTER_TM)
    tri = jnp.asarray(np.triu(np.ones((tm, tm), np.float32)), dtype=MXU_DTYPE)
    kt = lambda dt: jax.ShapeDtypeStruct((TOP_K, t), dt)
    col = pl.BlockSpec((TOP_K, tm), lambda i: (0, i))
    eidx, gates, rank, cnt = pl.pallas_call(
        _router_kernel,
        out_shape=(kt(jnp.int32), kt(jnp.float32), kt(jnp.int32), jax.ShapeDtypeStruct((e, 1), jnp.float32)),
        grid=(t // tm,),
        in_specs=[pl.BlockSpec((e, tm), lambda i: (0, i)),
                  pl.BlockSpec((e, 1), lambda i: (0, 0)),
                  pl.BlockSpec((tm, tm), lambda i: (0, 0))],
        out_specs=(col, col, col, pl.BlockSpec((e, 1), lambda i: (0, 0))),
        scratch_shapes=[pltpu.VMEM((e, 1), jnp.float32)],
        compiler_params=_cparams("arbitrary"),
        name="moe_router",
    )(logits_t, router_bias.astype(jnp.float32).reshape(e, 1), tri)
    return eidx, gates, rank, cnt[:, 0].astype(jnp.int32)


def _moe_routed(x1, eidx, rank, counts, w1, w3, w2):
    n_tok, d = x1.shape
    tk = n_tok * TOP_K
    n_blocks = (tk + N_EXPERTS * (EXPERT_BLOCK - 1)) // EXPERT_BLOCK + 1
    n_slots = n_blocks * EXPERT_BLOCK
    padded = (counts + EXPERT_BLOCK - 1) // EXPERT_BLOCK * EXPERT_BLOCK
    pad_end = jnp.cumsum(padded)
    start_pad = (pad_end - padded).astype(jnp.int32)
    dest = jnp.take(start_pad, eidx, axis=0) + rank
    tok = jnp.broadcast_to(jnp.arange(n_tok, dtype=jnp.int32)[None, :], (TOP_K, n_tok))
    slot_tok = jnp.full((n_slots,), n_tok, jnp.int32).at[dest.reshape(tk)].set(
        tok.reshape(tk), unique_indices=True, mode="drop")
    blk_e = jnp.minimum(jnp.searchsorted(pad_end, jnp.arange(n_blocks, dtype=jnp.int32) * EXPERT_BLOCK,
                                         side='right'), N_EXPERTS - 1).astype(jnp.int32)
    xpad = jnp.concatenate([x1.astype(MXU_DTYPE), jnp.zeros((1, d), MXU_DTYPE)], axis=0)
    xs = xpad[slot_tok]
    y = _grouped_experts(xs, blk_e, w1, w3, w2)
    return y, dest


def _post_kernel(x1_ref, sh_ref, rt_ref, p_ref, g_ref, b_ref, wp_ref, ng_ref, wg_ref, o_ref, *, alpha):
    y = alpha * x1_ref[...] + sh_ref[...] + rt_ref[...]
    x2 = _layer_norm(y, g_ref[...], b_ref[...])
    e = _dot(p_ref[...].astype(MXU_DTYPE), wp_ref[...])
    e = e * lax.rsqrt(jnp.mean(e * e, axis=-1, keepdims=True) + RMS_EPS) * ng_ref[...]
    gate = _sigmoid(_dot(x2.astype(MXU_DTYPE), wg_ref[...]))
    o_ref[...] = x2 + gate * e


def _post_block(x1, shared, routed, p, ln_g, ln_b, w_proj, norm_g, w_gate, alpha):
    t, d = x1.shape
    pd = p.shape[1]
    tm = _pick_tm(t, 512)
    row = lambda n: pl.BlockSpec((tm, n), lambda i: (i, 0))
    res = lambda a, c: pl.BlockSpec((a, c), lambda i: (0, 0))
    return pl.pallas_call(
        functools.partial(_post_kernel, alpha=alpha),
        out_shape=jax.ShapeDtypeStruct((t, d), jnp.float32),
        grid=(t // tm,),
        in_specs=[row(d), row(d), row(d), row(pd), res(1, d), res(1, d), res(pd, d), res(1, d), res(d, d)],
        out_specs=row(d),
        compiler_params=_cparams("parallel"),
        name="moe_combine_ln_ple",
    )(x1, shared, routed, p, ln_g.reshape(1, d), ln_b.reshape(1, d), w_proj.astype(MXU_DTYPE),
      norm_g.reshape(1, d), w_gate.astype(MXU_DTYPE))


def _rope_tables(s_len):
    inv = 1.0 / (ROPE_THETA ** (np.arange(0, HEAD_DIM, 2, dtype=np.float32) / HEAD_DIM))
    ang = jnp.arange(s_len, dtype=jnp.float32)[:, None] * jnp.asarray(inv, dtype=jnp.float32)[None, :]
    return jnp.cos(ang), jnp.sin(ang)


def _rope(x, cos, sin):
    b, l, w = x.shape
    xh = x.reshape(b, l, w // HEAD_DIM, HEAD_DIM)
    half = HEAD_DIM // 2
    x1, x2 = xh[..., :half], xh[..., half:]
    c = cos[:, None, :]
    s = sin[:, None, :]
    return jnp.concatenate([x1 * c - x2 * s, x2 * c + x1 * s], axis=-1).reshape(b, l, w)


def _split_cols(h, sizes):
    out, c = [], 0
    for n in sizes:
        out.append(h[..., c:c + n])
        c += n
    return out


def _even_mixer(x, w_in, b_gate, pos_k, w1_k, w2_k, pos_v, w1_v, w2_v, cos, sin):
    b, s, d = x.shape
    mix = A_HEADS * HEAD_DIM
    splits = (mix, HEAD_DIM, HEAD_DIM, IDX_HEADS * IDX_DIM, IDX_DIM, IDX_HEADS, B_HEADS * HEAD_DIM,
              6 * HEAD_DIM, 3 * B_HEADS)
    h = _matmul(x.reshape(b * s, d), w_in, name="even_in_proj").reshape(b, s, -1)
    q_a, k_a, v_a, q_i, k_i, w_i, q_b, kv_b, g_b = _split_cols(h, splits)
    o_a = _dsa_attention(_rope(q_a, cos, sin), _rope(k_a, cos, sin), v_a,
                         _rope(q_i, cos, sin), _rope(k_i, cos, sin), w_i)
    k_c, v_c, k_s, v_s, k_w, v_w = _split_cols(kv_b, (HEAD_DIM,) * 6)
    nc = s // CMP_STRIDE
    c_last = jnp.minimum(jnp.arange(nc) * CMP_STRIDE + CMP_LEN - 1, s - 1)
    kc = _rope(_compress(k_c, pos_k, w1_k, w2_k), cos[c_last], sin[c_last])
    vc = _compress(v_c, pos_v, w1_v, w2_v)
    gate = (g_b + b_gate).reshape(b, s, B_HEADS, 3)
    o_b = _nsa_attention(_rope(q_b, cos, sin), kc, vc, _rope(k_s, cos, sin), v_s,
                         _rope(k_w, cos, sin), v_w, gate)
    return jnp.concatenate([o_a, o_b], axis=-1).reshape(b * s, 2 * mix)


def _odd_mixer(x, w_in, b_f):
    b, s, d = x.shape
    mix = C_HEADS * HEAD_DIM
    h = _matmul(x.reshape(b * s, d), w_in, name="odd_in_proj").reshape(b, s, -1)
    q, k, v, f = _split_cols(h, (mix, mix, mix, C_HEADS))
    log_f = jax.nn.log_sigmoid((f + b_f).astype(jnp.float32))
    return _fox_attention(q, k, v, log_f).reshape(b * s, mix)


def kernel(x, p, ev_w_in, ev_b_gate, ev_cmp_pos_k, ev_cmp_w1_k, ev_cmp_w2_k, ev_cmp_pos_v, ev_cmp_w1_v, ev_cmp_w2_v, ev_w_o, od_w_in, od_b_f, od_w_o, ln1_g, ln1_b, ln2_g, ln2_b, router_w, router_bias, exp_w1, exp_w3, exp_w2, sh_w1, sh_w3, sh_w2, ple_w_gate, ple_w_proj, ple_norm_g):
    b, s, d = x.shape
    depth = p.shape[0]
    alpha = float((2.0 * depth) ** 0.25)
    cos, sin = _rope_tables(s)
    xf = x.reshape(b * s, d)
    for i in range(depth):
        j = i // 2
        x3 = xf.reshape(b, s, d)
        if i % 2 == 0:
            o = _even_mixer(x3, ev_w_in[j], ev_b_gate[j], ev_cmp_pos_k[j], ev_cmp_w1_k[j], ev_cmp_w2_k[j],
                            ev_cmp_pos_v[j], ev_cmp_w1_v[j], ev_cmp_w2_v[j], cos, sin)
            w_o = ev_w_o[j]
        else:
            o = _odd_mixer(x3, od_w_in[j], od_b_f[j])
            w_o = od_w_o[j]
        x1, shared, logits_t = _mid_block(o, xf, w_o, ln1_g[i], ln1_b[i], sh_w1[i], sh_w3[i], sh_w2[i],
                                          router_w[i], alpha)
        eidx, gates, rank, counts = _route(logits_t, router_bias[i])
        y, dest = _moe_routed(x1, eidx, rank, counts, exp_w1[i], exp_w3[i], exp_w2[i])
        routed = jnp.sum(y[dest.T] * gates.T[:, :, None], axis=1)
        xf = _post_block(x1, shared, routed, p[i].reshape(b * s, -1), ln2_g[i], ln2_b[i],
                         ple_w_proj[i], ple_norm_g[i], ple_w_gate[i], alpha)
    return xf.reshape(b, s, d)
```

```python
import functools

import numpy as np
import jax
import jax.numpy as jnp
from jax import lax
from jax.experimental import pallas as pl
from jax.experimental.pallas import tpu as pltpu

HEAD_DIM = 64
QBLOCK = 128
ROPE_THETA = 10000.0
LN_EPS = 1e-5
RMS_EPS = 1e-6
NEG = -1e30
BIG = 1e30

A_HEADS = 8
IDX_HEADS = 8
IDX_DIM = 64
DSA_TOPK_MAX = 256

B_HEADS = 8
CMP_LEN = 32
CMP_STRIDE = 16
SEL_LEN = 32
SEL_BLOCKS_MAX = 8
WINDOW = 512

C_HEADS = 16

N_EXPERTS = 64
TOP_K = 8
N_GROUPS = 8
TOPK_GROUPS = 4
ROUTED_SCALE = 2.5
EXPERT_BLOCK = 256

LANES = 128
SUBLANES = 8
VMEM_LIMIT = 48 * 1024 * 1024

MXU_DTYPE = jnp.bfloat16
KEY_TILE = 256
ATTN_SCALE = HEAD_DIM ** -0.5
assert ATTN_SCALE == 2.0 ** round(np.log2(ATTN_SCALE))
INT_MIN = -(2 ** 31)
IDX_ALL = 2 ** 30


def _cparams(*sem):
    return pltpu.CompilerParams(dimension_semantics=sem, vmem_limit_bytes=VMEM_LIMIT)


def _dot(a, b):
    return jnp.dot(a, b, preferred_element_type=jnp.float32)


def _dot_nt(a, b):
    return lax.dot_general(a, b, (((1,), (1,)), ((), ())), preferred_element_type=jnp.float32)


def _silu(x):
    return x * (1.0 / (1.0 + jnp.exp(-x)))


def _sigmoid(x):
    return 1.0 / (1.0 + jnp.exp(-x))


def _mm_kernel(x_ref, w_ref, o_ref):
    o_ref[...] = _dot(x_ref[...].astype(MXU_DTYPE), w_ref[...])


def _pick_tm(t, cap):
    tm = min(cap, t)
    while t % tm:
        tm //= 2
    return tm


def _matmul(x, w, *, tm_cap=512, name="proj"):
    t, k = x.shape
    n = w.shape[1]
    n_pad = -(-n // LANES) * LANES
    w = jnp.pad(w, ((0, 0), (0, n_pad - n))).astype(MXU_DTYPE)
    tm = _pick_tm(t, tm_cap)
    out = pl.pallas_call(
        _mm_kernel,
        out_shape=jax.ShapeDtypeStruct((t, n_pad), jnp.float32),
        grid=(t // tm,),
        in_specs=[pl.BlockSpec((tm, k), lambda i: (i, 0)),
                  pl.BlockSpec((k, n_pad), lambda i: (0, 0))],
        out_specs=pl.BlockSpec((tm, n_pad), lambda i: (i, 0)),
        compiler_params=_cparams("parallel"),
        name=name,
    )(x, w)
    return out


def _online_update(s, drop, vt, m_ref, l_ref, acc_ref, idx=Ellipsis):
    if drop is not None:
        s = s + drop
    m_old = m_ref[idx]
    m_new = jnp.maximum(m_old, jnp.max(s, axis=0, keepdims=True))
    p = jnp.exp(s - m_new)
    alpha = jnp.exp(m_old - m_new)
    l_ref[idx] = alpha * l_ref[idx] + jnp.sum(p, axis=0, keepdims=True)
    acc_ref[idx] = alpha * acc_ref[idx] + _dot(vt, p.astype(vt.dtype))
    m_ref[idx] = m_new


def _drop(pred, reps=1):
    d = jnp.where(pred, 0.0, NEG)
    return d if reps == 1 else jnp.concatenate([d] * reps, axis=1)


def _init_state(m_ref, l_ref, acc_ref):
    m_ref[...] = jnp.full(m_ref.shape, NEG, jnp.float32)
    l_ref[...] = jnp.zeros(l_ref.shape, jnp.float32)
    acc_ref[...] = jnp.zeros(acc_ref.shape, jnp.float32)


def _normalized(l_ref, acc_ref, idx=Ellipsis):
    return acc_ref[idx] / l_ref[idx]


def _pipelined_tiles(lo, hi, produce, consume, consume_last=None):
    consume_last = consume_last or consume
    produce(lo, 0)

    def pair(j, carry):
        kt = lo + 2 * j
        produce(kt + 1, 1)
        consume(kt, 0)
        produce(kt + 2, 0)
        consume(kt + 1, 1)
        return carry

    n_body = hi - lo - 1
    lax.fori_loop(0, n_body // 2, pair, 0)

    @pl.when(n_body % 2 == 0)
    def _():
        consume_last(hi - 1, 0)

    @pl.when(n_body % 2 == 1)
    def _():
        produce(hi - 1, 1)
        consume(hi - 2, 0)
        consume_last(hi - 1, 1)


def _store_heads(o_t, o_ref, n_heads, tq):
    for j in range(n_heads // 2):
        pair = jnp.concatenate([o_t[:, (2 * j) * tq:(2 * j + 1) * tq],
                                o_t[:, (2 * j + 1) * tq:(2 * j + 2) * tq]], axis=0)
        o_ref[0, :, 2 * j * HEAD_DIM:(2 * j + 2) * HEAD_DIM] = pair.T.astype(o_ref.dtype)


def _dsa_kernel(qi_ref, w_ref, ki_ref, qa_ref, ka_ref, vat_ref, o_ref,
                key_ref, m_ref, l_ref, acc_ref, sbuf_ref, *, k_sel, idx_scale):
    i = pl.program_id(1)
    tq = QBLOCK
    tk = KEY_TILE
    n_tiles = (i * tq + tq + tk - 1) // tk
    q0 = i * tq
    t_row = q0 + lax.broadcasted_iota(jnp.int32, (1, tq), 1)
    kk = lax.broadcasted_iota(jnp.int32, (tk, tq), 0)

    qi = qi_ref[0].reshape(IDX_HEADS * tq, IDX_DIM)
    w_row = w_ref[0, 0]

    def tile_start(kt):
        return pl.multiple_of(kt * tk, tk)

    def logits_tile(kt, slot):
        sbuf_ref[slot] = _dot_nt(ki_ref[0, pl.ds(tile_start(kt), tk), :], qi)

    def score_tile(kt, slot):
        k0 = tile_start(kt)
        z = jnp.maximum(sbuf_ref[slot], 0.0) * w_row
        sc = z[:, 0:tq]
        for h in range(1, IDX_HEADS):
            sc = sc + z[:, h * tq:(h + 1) * tq]
        sc = sc * idx_scale
        sc = jnp.where(k0 + kk <= t_row, sc, NEG)
        bits = lax.bitcast_convert_type(sc, jnp.int32)
        key_ref[pl.ds(k0, tk), :] = bits ^ ((bits >> 31) & 0x7FFFFFFF)

    _pipelined_tiles(0, n_tiles, logits_tile, score_tile)

    def count(pred_fn):
        def body(kt, acc):
            k0 = pl.multiple_of(kt * tk, tk)
            c = pred_fn(key_ref[pl.ds(k0, tk), :], k0 + kk).astype(jnp.int32)
            return acc + jnp.sum(c.reshape(tk // SUBLANES, SUBLANES, tq), axis=0)
        acc = lax.fori_loop(0, n_tiles, body, jnp.zeros((SUBLANES, tq), jnp.int32))
        return jnp.sum(acc, axis=0, keepdims=True)

    def search(_):
        c0 = count(lambda key, idx: key >= 0)
        thr = jnp.where(c0 >= k_sel, 0, INT_MIN).astype(jnp.int32)

        def bit_step(b, thr):
            trial = thr | (jnp.int32(1) << (30 - b))
            c = count(lambda key, idx: key >= trial)
            return jnp.where(c >= k_sel, trial, thr)

        thr = lax.fori_loop(0, 31, bit_step, thr)
        c_ge = count(lambda key, idx: key >= thr)

        def tie_search(_):
            c_gt = count(lambda key, idx: key > thr)
            need = k_sel - c_gt

            def idx_step(b, u):
                trial = u | (jnp.int32(1) << (20 - b))
                c = count(lambda key, idx: (key == thr) & (idx < trial))
                return jnp.where(c < need, trial, u)

            return lax.fori_loop(0, 21, idx_step, jnp.zeros((1, tq), jnp.int32))

        cut = lax.cond(jnp.max(c_ge) > k_sel, tie_search,
                       lambda _: jnp.full((1, tq), IDX_ALL, jnp.int32), 0)
        return thr, cut

    thr, cut = lax.cond(q0 + tq > k_sel, search,
                        lambda _: (jnp.full((1, tq), INT_MIN, jnp.int32),
                                   jnp.full((1, tq), IDX_ALL, jnp.int32)), 0)

    qa = qa_ref[0].reshape(A_HEADS * tq, HEAD_DIM)
    _init_state(m_ref, l_ref, acc_ref)

    def qk_tile(kt, slot):
        sbuf_ref[slot] = _dot_nt(ka_ref[0, pl.ds(tile_start(kt), tk), :], qa)

    def attn_tile(kt, slot):
        k0 = tile_start(kt)
        key = key_ref[pl.ds(k0, tk), :]
        idx = k0 + kk
        sel = ((key > thr) | ((key == thr) & (idx <= cut))) & (idx <= t_row)
        _online_update(sbuf_ref[slot], _drop(sel, A_HEADS), vat_ref[0, kt], m_ref, l_ref, acc_ref)

    _pipelined_tiles(0, n_tiles, qk_tile, attn_tile)
    _store_heads(_normalized(l_ref, acc_ref), o_ref, A_HEADS, tq)


def _heads_major(x, n_heads, scale=1.0):
    b, s, _ = x.shape
    return jnp.transpose((x * scale).reshape(b, s, n_heads, HEAD_DIM), (0, 2, 1, 3)).astype(MXU_DTYPE)


def _value_tiles_t(v, tk):
    b, s, d = v.shape
    return jnp.transpose(v.reshape(b, s // tk, tk, d), (0, 1, 3, 2)).astype(MXU_DTYPE)


def _per_query_rows(x, tq):
    b, s, h = x.shape
    return jnp.transpose(x.reshape(b, s // tq, tq, h), (0, 1, 3, 2)).reshape(b, s // tq, 1, h * tq)


def _dsa_attention(q_a, k_a, v_a, q_i, k_i, w_i):
    b, s, _ = q_a.shape
    tq = QBLOCK
    nq = s // tq
    k_sel = min(DSA_TOPK_MAX, s // 4)
    tk = KEY_TILE
    kern = functools.partial(_dsa_kernel, k_sel=k_sel, idx_scale=float((IDX_HEADS * IDX_DIM) ** -0.5))
    hq = A_HEADS * tq
    return pl.pallas_call(
        kern,
        out_shape=jax.ShapeDtypeStruct((b, s, A_HEADS * HEAD_DIM), MXU_DTYPE),
        grid=(b, nq),
        in_specs=[
            pl.BlockSpec((1, IDX_HEADS, tq, IDX_DIM), lambda bi, i: (bi, 0, i, 0)),
            pl.BlockSpec((1, 1, 1, hq), lambda bi, i: (bi, i, 0, 0)),
            pl.BlockSpec((1, s, IDX_DIM), lambda bi, i: (bi, 0, 0)),
            pl.BlockSpec((1, A_HEADS, tq, HEAD_DIM), lambda bi, i: (bi, 0, i, 0)),
            pl.BlockSpec((1, s, HEAD_DIM), lambda bi, i: (bi, 0, 0)),
            pl.BlockSpec((1, s // tk, HEAD_DIM, tk), lambda bi, i: (bi, 0, 0, 0)),
        ],
        out_specs=pl.BlockSpec((1, tq, A_HEADS * HEAD_DIM), lambda bi, i: (bi, i, 0)),
        scratch_shapes=[pltpu.VMEM((s, tq), jnp.int32),
                        pltpu.VMEM((1, hq), jnp.float32),
                        pltpu.VMEM((1, hq), jnp.float32),
                        pltpu.VMEM((HEAD_DIM, hq), jnp.float32),
                        pltpu.VMEM((2, tk, hq), jnp.float32)],
        compiler_params=_cparams("parallel", "arbitrary"),
        name="dsa_attention",
    )(_heads_major(q_i, IDX_HEADS), _per_query_rows(w_i.astype(jnp.float32), tq),
      k_i.astype(MXU_DTYPE), _heads_major(q_a, A_HEADS, ATTN_SCALE), k_a.astype(MXU_DTYPE),
      _value_tiles_t(v_a, tk))


def _cmp_kernel(ck_ref, pe_ref, w1_ref, w2_ref, o_ref):
    nc = ck_ref.shape[1]
    half = ck_ref.shape[2]
    ck = ck_ref[0]
    a = _dot((ck + pe_ref[0:1, :]).astype(MXU_DTYPE), w1_ref[0:half, :])
    bm = _dot((ck + pe_ref[1:2, :]).astype(MXU_DTYPE), w1_ref[half:2 * half, :])
    h = a + pltpu.roll(bm, nc - 1, 0)
    o_ref[0] = _dot(_silu(h).astype(MXU_DTYPE), w2_ref[...])


def _compress(kv, pe, w1, w2):
    b, s, d = kv.shape
    nc = s // CMP_STRIDE
    half = CMP_STRIDE * d
    ck = kv.reshape(b, nc, half)
    pe2 = pe.reshape(2, half)
    return pl.pallas_call(
        _cmp_kernel,
        out_shape=jax.ShapeDtypeStruct((b, nc, d), jnp.float32),
        grid=(b,),
        in_specs=[pl.BlockSpec((1, nc, half), lambda bi: (bi, 0, 0)),
                  pl.BlockSpec((2, half), lambda bi: (0, 0)),
                  pl.BlockSpec((2 * half, d), lambda bi: (0, 0)),
                  pl.BlockSpec((d, d), lambda bi: (0, 0))],
        out_specs=pl.BlockSpec((1, nc, d), lambda bi: (bi, 0, 0)),
        compiler_params=_cparams("parallel"),
        name="nsa_compress",
    )(ck, pe2, w1.astype(MXU_DTYPE), w2.astype(MXU_DTYPE))


def _split3(x):
    def top(v):
        return lax.bitcast_convert_type(lax.bitcast_convert_type(v, jnp.int32) & jnp.int32(-65536), jnp.float32)
    hi = top(x)
    r1 = x - hi
    mid = top(r1)
    lo = r1 - mid
    return hi.astype(MXU_DTYPE), mid.astype(MXU_DTYPE), lo.astype(MXU_DTYPE)


def _nsa_kernel(q_ref, g_ref, kc_ref, vct_ref, ovt_ref, exp_ref, ks_ref, vst_ref, kw_ref, vwt_ref, o_ref,
                tok_ref, m_ref, l_ref, acc_ref, m2_ref, l2_ref, acc2_ref, out_ref, sbuf_ref, *, n_sel, n_cmp):
    i = pl.program_id(1)
    tq = QBLOCK
    tk = KEY_TILE
    nh = B_HEADS
    hq = nh * tq
    q0 = i * tq
    n_tiles = (q0 + tq + tk - 1) // tk
    t_row = q0 + lax.broadcasted_iota(jnp.int32, (1, tq), 1)
    kk = lax.broadcasted_iota(jnp.int32, (tk, tq), 0)
    q = q_ref[0].reshape(hq, HEAD_DIM)
    gates = g_ref[0, 0]

    nc = kc_ref.shape[1]
    c_id = lax.broadcasted_iota(jnp.int32, (nc, tq), 0)
    valid_c = (c_id * CMP_STRIDE + (CMP_LEN - 1) <= t_row) & (c_id < n_cmp)
    keep_c = jnp.concatenate([jnp.where(valid_c, 1.0, 0.0)] * nh, axis=1)
    s_c = _dot_nt(kc_ref[0], q) + _drop(valid_c, nh)
    e_c = jnp.exp(s_c - jnp.max(s_c, axis=0, keepdims=True)) * keep_c
    den = jnp.sum(e_c, axis=0, keepdims=True)
    p_c = e_c / jnp.where(den > 0.0, den, 1.0)
    out_ref[...] = gates[0:1, :] * _dot(vct_ref[0], p_c.astype(MXU_DTYPE))

    p_sum = p_c[:, 0:tq]
    for h in range(1, nh):
        p_sum = p_sum + p_c[:, h * tq:(h + 1) * tq]
    ovt = ovt_ref[...]
    pieces = _split3(p_sum)
    imp = _dot(ovt, pieces[0]) + _dot(ovt, pieces[1]) + _dot(ovt, pieces[2])
    n_sb = ovt.shape[0]
    j_id = lax.broadcasted_iota(jnp.int32, (n_sb, tq), 0)
    cur = t_row >> (SEL_LEN.bit_length() - 1)
    forced = (j_id == 0) | (j_id == cur) | (j_id == cur - 1)
    future = j_id * SEL_LEN > t_row
    imp = jnp.where(forced, BIG, jnp.where(future, NEG, imp))
    chosen = jnp.zeros((n_sb, tq), jnp.bool_)
    for _ in range(n_sel):
        cand = jnp.where(chosen, -jnp.inf, imp)
        best = jnp.max(cand, axis=0, keepdims=True)
        first = jnp.min(jnp.where(cand == best, j_id, n_sb), axis=0, keepdims=True)
        chosen = chosen | (j_id == first)
    tok_ref[...] = _dot(exp_ref[...], jnp.where(chosen, 1.0, 0.0).astype(MXU_DTYPE))

    _init_state(m_ref, l_ref, acc_ref)
    _init_state(m2_ref, l2_ref, acc2_ref)
    win_lo = jnp.maximum(q0 - WINDOW, 0) // tk

    def tile_start(kt):
        return pl.multiple_of(kt * tk, tk)

    def sel_qk(kt, slot):
        sbuf_ref[slot] = _dot_nt(ks_ref[0, pl.ds(tile_start(kt), tk), :], q)

    def sel_tile(kt, slot):
        k0 = tile_start(kt)
        sel = (tok_ref[pl.ds(k0, tk), :] > 0.5) & (k0 + kk <= t_row)
        _online_update(sbuf_ref[slot], _drop(sel, nh), vst_ref[0, kt], m_ref, l_ref, acc_ref)

    def win_qk(kt, slot):
        sbuf_ref[slot] = _dot_nt(kw_ref[0, pl.ds(tile_start(kt), tk), :], q)

    def win_tile(kt, slot):
        kpos = tile_start(kt) + kk
        ok = (kpos <= t_row) & (kpos > t_row - WINDOW)
        _online_update(sbuf_ref[slot], _drop(ok, nh), vwt_ref[0, kt], m2_ref, l2_ref, acc2_ref)

    _pipelined_tiles(0, n_tiles, sel_qk, sel_tile)
    _pipelined_tiles(win_lo, n_tiles, win_qk, win_tile)
    o_t = out_ref[...] + gates[1:2, :] * _normalized(l_ref, acc_ref) + gates[2:3, :] * _normalized(l2_ref, acc2_ref)
    _store_heads(o_t, o_ref, nh, tq)


def _nsa_attention(q_b, k_c, v_c, k_s, v_s, k_w, v_w, gate):
    b, s, _ = q_b.shape
    tq = QBLOCK
    nq = s // tq
    nh = B_HEADS
    hq = nh * tq
    nc = k_c.shape[1]
    n_cmp = (s - CMP_LEN) // CMP_STRIDE + 1
    n_sb = s // SEL_LEN
    n_sel = min(SEL_BLOCKS_MAX, n_sb)
    c_start = np.arange(nc) * CMP_STRIDE
    s_start = np.arange(n_sb) * SEL_LEN
    ov = np.clip(np.minimum(c_start[:, None] + CMP_LEN, s_start[None, :] + SEL_LEN)
                 - np.maximum(c_start[:, None], s_start[None, :]), 0, None) / CMP_LEN
    ov[n_cmp:] = 0.0
    ovt = jnp.asarray(ov.T, dtype=MXU_DTYPE)
    expand = jnp.asarray((np.arange(s)[:, None] // SEL_LEN == np.arange(n_sb)[None, :]), dtype=MXU_DTYPE)
    g = _sigmoid(gate.astype(jnp.float32))
    g = jnp.transpose(g.reshape(b, nq, tq, nh, 3), (0, 1, 4, 3, 2)).reshape(b, nq, 3, hq)
    tk = KEY_TILE
    kern = functools.partial(_nsa_kernel, n_sel=n_sel, n_cmp=n_cmp)
    full = lambda shape: pl.BlockSpec(shape, lambda bi, i: (0,) * len(shape))
    per_b = lambda shape: pl.BlockSpec(shape, lambda bi, i: (bi,) + (0,) * (len(shape) - 1))
    state = [pltpu.VMEM((1, hq), jnp.float32), pltpu.VMEM((1, hq), jnp.float32),
             pltpu.VMEM((HEAD_DIM, hq), jnp.float32)]
    return pl.pallas_call(
        kern,
        out_shape=jax.ShapeDtypeStruct((b, s, nh * HEAD_DIM), MXU_DTYPE),
        grid=(b, nq),
        in_specs=[
            pl.BlockSpec((1, nh, tq, HEAD_DIM), lambda bi, i: (bi, 0, i, 0)),
            pl.BlockSpec((1, 1, 3, hq), lambda bi, i: (bi, i, 0, 0)),
            per_b((1, nc, HEAD_DIM)),
            per_b((1, HEAD_DIM, nc)),
            full((n_sb, nc)),
            full((s, n_sb)),
            per_b((1, s, HEAD_DIM)),
            per_b((1, s // tk, HEAD_DIM, tk)),
            per_b((1, s, HEAD_DIM)),
            per_b((1, s // tk, HEAD_DIM, tk)),
        ],
        out_specs=pl.BlockSpec((1, tq, nh * HEAD_DIM), lambda bi, i: (bi, i, 0)),
        scratch_shapes=[pltpu.VMEM((s, tq), jnp.float32)] + state + state
                       + [pltpu.VMEM((HEAD_DIM, hq), jnp.float32), pltpu.VMEM((2, tk, hq), jnp.float32)],
        compiler_params=_cparams("parallel", "arbitrary"),
        name="nsa_attention",
    )(_heads_major(q_b, nh, ATTN_SCALE), g, k_c.astype(MXU_DTYPE),
      jnp.transpose(v_c, (0, 2, 1)).astype(MXU_DTYPE), ovt, expand,
      k_s.astype(MXU_DTYPE), _value_tiles_t(v_s, tk), k_w.astype(MXU_DTYPE), _value_tiles_t(v_w, tk))


FOX_TQ = 256
FOX_HB = 4
FOX_KPAD = 128


def _fox_kernel(q_ref, k_ref, vt_ref, o_ref, m_ref, l_ref, acc_ref, sbuf_ref):
    i = pl.program_id(2)
    tq = FOX_TQ
    _init_state(m_ref, l_ref, acc_ref)

    def qk_tile(kt, slot):
        k0 = pl.multiple_of(kt * tq, tq)
        for hh in range(FOX_HB):
            sbuf_ref[slot, hh] = _dot_nt(k_ref[0, hh, pl.ds(k0, tq), :], q_ref[0, hh])

    def update(kt, slot, drop):
        for hh in range(FOX_HB):
            _online_update(sbuf_ref[slot, hh], drop, vt_ref[0, hh, kt], m_ref, l_ref, acc_ref, idx=hh)

    def diag_tile(kt, slot):
        kk = lax.broadcasted_iota(jnp.int32, (tq, tq), 0)
        qq = lax.broadcasted_iota(jnp.int32, (tq, tq), 1)
        update(kt, slot, _drop(kk <= qq))

    _pipelined_tiles(0, i + 1, qk_tile, lambda kt, slot: update(kt, slot, None), diag_tile)
    for j in range(FOX_HB // 2):
        pair = jnp.concatenate([_normalized(l_ref, acc_ref, 2 * j), _normalized(l_ref, acc_ref, 2 * j + 1)], axis=0)
        o_ref[0, :, 2 * j * HEAD_DIM:(2 * j + 2) * HEAD_DIM] = pair.T.astype(o_ref.dtype)


def _fox_attention(q, k, v, log_f):
    b, s, _ = q.shape
    nh = C_HEADS
    tq = min(FOX_TQ, s)
    assert tq == FOX_TQ
    d_cum = jnp.cumsum(log_f, axis=1)
    d3 = jnp.stack(_split3(d_cum), axis=-1)
    ones = jnp.ones((b, s, nh, 3), MXU_DTYPE)
    pad = jnp.zeros((b, s, nh, FOX_KPAD - HEAD_DIM - 3), MXU_DTYPE)
    qh = (q * ATTN_SCALE).reshape(b, s, nh, HEAD_DIM).astype(MXU_DTYPE)
    kh = k.reshape(b, s, nh, HEAD_DIM).astype(MXU_DTYPE)
    q_aug = jnp.transpose(jnp.concatenate([qh, ones, pad], axis=-1), (0, 2, 1, 3))
    k_aug = jnp.transpose(jnp.concatenate([kh, -d3, pad], axis=-1), (0, 2, 1, 3))
    vt = jnp.transpose(v.reshape(b, s // tq, tq, nh, HEAD_DIM), (0, 3, 1, 4, 2)).astype(MXU_DTYPE)
    return pl.pallas_call(
        _fox_kernel,
        out_shape=jax.ShapeDtypeStruct((b, s, nh * HEAD_DIM), MXU_DTYPE),
        grid=(b, nh // FOX_HB, s // tq),
        in_specs=[
            pl.BlockSpec((1, FOX_HB, tq, FOX_KPAD), lambda bi, j, i: (bi, j, i, 0)),
            pl.BlockSpec((1, FOX_HB, s, FOX_KPAD), lambda bi, j, i: (bi, j, 0, 0)),
            pl.BlockSpec((1, FOX_HB, s // tq, HEAD_DIM, tq), lambda bi, j, i: (bi, j, 0, 0, 0)),
        ],
        out_specs=pl.BlockSpec((1, tq, FOX_HB * HEAD_DIM), lambda bi, j, i: (bi, i, j)),
        scratch_shapes=[pltpu.VMEM((FOX_HB, 1, tq), jnp.float32),
                        pltpu.VMEM((FOX_HB, 1, tq), jnp.float32),
                        pltpu.VMEM((FOX_HB, HEAD_DIM, tq), jnp.float32),
                        pltpu.VMEM((2, FOX_HB, tq, tq), jnp.float32)],
        compiler_params=_cparams("parallel", "parallel", "arbitrary"),
        name="fox_attention",
    )(q_aug, k_aug, vt)


def _layer_norm(y, g, b):
    mu = jnp.mean(y, axis=-1, keepdims=True)
    d = y - mu
    var = jnp.mean(d * d, axis=-1, keepdims=True)
    return d * lax.rsqrt(var + LN_EPS) * g + b


def _mid_kernel(o_ref, x_ref, wo_ref, g_ref, b_ref, sw1_ref, sw3_ref, sw2_ref, rw_ref,
                x1_ref, sh_ref, lg_ref, *, alpha):
    y = alpha * x_ref[...] + _dot(o_ref[...], wo_ref[...])
    x1 = _layer_norm(y, g_ref[...], b_ref[...])
    x1_ref[...] = x1
    xb = x1.astype(MXU_DTYPE)
    hid = _silu(_dot(xb, sw1_ref[...])) * _dot(xb, sw3_ref[...])
    sh_ref[...] = _dot(hid.astype(MXU_DTYPE), sw2_ref[...])
    lg_ref[...] = _dot_nt(rw_ref[...], xb)


def _mid_block(o, x, w_o, ln_g, ln_b, sw1, sw3, sw2, router_w, alpha):
    t, d = x.shape
    f = sw1.shape[1]
    e = router_w.shape[1]
    tm = _pick_tm(t, 512)
    row = lambda n: pl.BlockSpec((tm, n), lambda i: (i, 0))
    res = lambda a, c: pl.BlockSpec((a, c), lambda i: (0, 0))
    return pl.pallas_call(
        functools.partial(_mid_kernel, alpha=alpha),
        out_shape=(jax.ShapeDtypeStruct((t, d), jnp.float32),
                   jax.ShapeDtypeStruct((t, d), jnp.float32),
                   jax.ShapeDtypeStruct((e, t), jnp.float32)),
        grid=(t // tm,),
        in_specs=[row(o.shape[1]), row(d), res(o.shape[1], d), res(1, d), res(1, d),
                  res(d, f), res(d, f), res(f, d), res(e, d)],
        out_specs=(row(d), row(d), pl.BlockSpec((e, tm), lambda i: (0, i))),
        compiler_params=_cparams("parallel"),
        name="mixer_out_ln_shared_router",
    )(o, x, w_o.astype(MXU_DTYPE), ln_g.reshape(1, d), ln_b.reshape(1, d),
      sw1.astype(MXU_DTYPE), sw3.astype(MXU_DTYPE), sw2.astype(MXU_DTYPE), router_w.T.astype(MXU_DTYPE))


def _expert_kernel(blk_e_ref, xs_ref, w1_ref, w3_ref, w2_ref, y_ref):
    xe = xs_ref[...]
    hid = _silu(_dot(xe, w1_ref[0])) * _dot(xe, w3_ref[0])
    y_ref[...] = _dot(hid.astype(MXU_DTYPE), w2_ref[0])


def _grouped_experts(xs, blk_e, w1, w3, w2):
    n_slots, d = xs.shape
    f = w1.shape[2]
    n_blocks = n_slots // EXPERT_BLOCK
    return pl.pallas_call(
        _expert_kernel,
        out_shape=jax.ShapeDtypeStruct((n_slots, d), jnp.float32),
        grid_spec=pltpu.PrefetchScalarGridSpec(
            num_scalar_prefetch=1,
            grid=(n_blocks,),
            in_specs=[pl.BlockSpec((EXPERT_BLOCK, d), lambda i, be: (i, 0)),
                      pl.BlockSpec((1, d, f), lambda i, be: (be[i], 0, 0)),
                      pl.BlockSpec((1, d, f), lambda i, be: (be[i], 0, 0)),
                      pl.BlockSpec((1, f, d), lambda i, be: (be[i], 0, 0))],
            out_specs=pl.BlockSpec((EXPERT_BLOCK, d), lambda i, be: (i, 0)),
        ),
        compiler_params=_cparams("arbitrary"),
        name="routed_experts",
    )(blk_e, xs, w1.astype(MXU_DTYPE), w3.astype(MXU_DTYPE), w2.astype(MXU_DTYPE))


ROUTER_TM = 512


def _pick_rows(rows, row_id, n):
    out = jnp.zeros((n,) + rows[0].shape[1:], rows[0].dtype)
    for r in range(n):
        out = jnp.where(row_id == r, rows[r], out)
    return out


def _router_kernel(lg_ref, bias_ref, tri_ref, eidx_ref, gate_ref, rank_ref, cnt_ref, carry_ref):
    e, tm = lg_ref.shape
    per_group = e // N_GROUPS

    @pl.when(pl.program_id(0) == 0)
    def _():
        carry_ref[...] = jnp.zeros(carry_ref.shape, jnp.float32)

    s = _sigmoid(lg_ref[...])
    sb = s + bias_ref[...]
    neg_inf = -jnp.inf

    sub_id = lax.broadcasted_iota(jnp.int32, (per_group, tm), 0)
    g_rows = []
    for g in range(N_GROUPS):
        blk = sb[g * per_group:(g + 1) * per_group, :]
        m1 = jnp.max(blk, axis=0, keepdims=True)
        f1 = jnp.min(jnp.where(blk == m1, sub_id, per_group), axis=0, keepdims=True)
        m2 = jnp.max(jnp.where(sub_id == f1, neg_inf, blk), axis=0, keepdims=True)
        g_rows.append(m1 + m2)
    g_id = lax.broadcasted_iota(jnp.int32, (N_GROUPS, tm), 0)
    gscore = _pick_rows(g_rows, g_id, N_GROUPS)

    e_id = lax.broadcasted_iota(jnp.int32, (e, tm), 0)
    e_group = e_id // per_group if per_group & (per_group - 1) else e_id >> (per_group.bit_length() - 1)
    g_taken = jnp.zeros((N_GROUPS, tm), jnp.bool_)
    e_allowed = jnp.zeros((e, tm), jnp.bool_)
    for _ in range(TOPK_GROUPS):
        cand = jnp.where(g_taken, neg_inf, gscore)
        best = jnp.max(cand, axis=0, keepdims=True)
        first = jnp.min(jnp.where(cand == best, g_id, N_GROUPS), axis=0, keepdims=True)
        g_taken = g_taken | (g_id == first)
        e_allowed = e_allowed | (e_group == first)

    masked = jnp.where(e_allowed, sb, NEG)
    chosen = jnp.zeros((e, tm), jnp.bool_)
    id_rows, sel_rows = [], []
    for _ in range(TOP_K):
        cand = jnp.where(chosen, neg_inf, masked)
        best = jnp.max(cand, axis=0, keepdims=True)
        first = jnp.min(jnp.where(cand == best, e_id, e), axis=0, keepdims=True)
        hit = e_id == first
        chosen = chosen | hit
        id_rows.append(first)
        sel_rows.append(jnp.sum(jnp.where(hit, s, 0.0), axis=0, keepdims=True))
    k_id = lax.broadcasted_iota(jnp.int32, (TOP_K, tm), 0)
    eidx = _pick_rows(id_rows, k_id, TOP_K)
    sel = _pick_rows(sel_rows, k_id, TOP_K)
    eidx_ref[...] = eidx
    gate_ref[...] = sel / jnp.sum(sel, axis=0, keepdims=True) * ROUTED_SCALE

    chosen_f = jnp.where(chosen, 1.0, 0.0)
    incl = _dot(chosen_f.astype(MXU_DTYPE), tri_ref[...])
    rank_dense = carry_ref[...] + incl - chosen_f
    rank_rows = [jnp.sum(jnp.where(e_id == id_rows[r], rank_dense, 0.0), axis=0, keepdims=True)
                 for r in range(TOP_K)]
    rank_ref[...] = _pick_rows(rank_rows, k_id, TOP_K).astype(jnp.int32)
    carry_ref[...] = carry_ref[...] + jnp.sum(chosen_f, axis=1, keepdims=True)
    cnt_ref[...] = carry_ref[...]


def _route(logits_t, router_bias):
    e, t = logits_t.shape
    tm = _pick_tm(t, ROUTER_TM)
    tri = jnp.asarray(np.triu(np.ones((tm, tm), np.float32)), dtype=MXU_DTYPE)
    kt = lambda dt: jax.ShapeDtypeStruct((TOP_K, t), dt)
    col = pl.BlockSpec((TOP_K, tm), lambda i: (0, i))
    eidx, gates, rank, cnt = pl.pallas_call(
        _router_kernel,
        out_shape=(kt(jnp.int32), kt(jnp.float32), kt(jnp.int32), jax.ShapeDtypeStruct((e, 1), jnp.float32)),
        grid=(t // tm,),
        in_specs=[pl.BlockSpec((e, tm), lambda i: (0, i)),
                  pl.BlockSpec((e, 1), lambda i: (0, 0)),
                  pl.BlockSpec((tm, tm), lambda i: (0, 0))],
        out_specs=(col, col, col, pl.BlockSpec((e, 1), lambda i: (0, 0))),
        scratch_shapes=[pltpu.VMEM((e, 1), jnp.float32)],
        compiler_params=_cparams("arbitrary"),
        name="moe_router",
    )(logits_t, router_bias.astype(jnp.float32).reshape(e, 1), tri)
    return eidx, gates, rank, cnt[:, 0].astype(jnp.int32)


def _moe_routed(x1, eidx, rank, counts, w1, w3, w2):
    n_tok, d = x1.shape
    tk = n_tok * TOP_K
    n_blocks = (tk + N_EXPERTS * (EXPERT_BLOCK - 1)) // EXPERT_BLOCK + 1
    n_slots = n_blocks * EXPERT_BLOCK
    padded = (counts + EXPERT_BLOCK - 1) // EXPERT_BLOCK * EXPERT_BLOCK
    e_ids = jnp.arange(N_EXPERTS, dtype=jnp.int32)
    pad_end = jnp.sum(jnp.where(e_ids[None, :] <= e_ids[:, None], padded[None, :], 0), axis=1)
    start_pad = (pad_end - padded).astype(jnp.int32)
    start_of = jnp.sum(jnp.where(eidx[None] == e_ids[:, None, None], start_pad[:, None, None], 0), axis=0)
    dest = start_of + rank
    tok = jnp.broadcast_to(jnp.arange(n_tok, dtype=jnp.int32)[None, :], (TOP_K, n_tok))
    slot_tok = jnp.full((n_slots,), n_tok, jnp.int32).at[dest.reshape(tk)].set(
        tok.reshape(tk), unique_indices=True, mode="drop")
    blk_start = jnp.arange(n_blocks, dtype=jnp.int32) * EXPERT_BLOCK
    blk_e = jnp.minimum(jnp.sum((pad_end[None, :] <= blk_start[:, None]).astype(jnp.int32), axis=1),
                        N_EXPERTS - 1).astype(jnp.int32)
    xpad = jnp.concatenate([x1.astype(MXU_DTYPE), jnp.zeros((1, d), MXU_DTYPE)], axis=0)
    xs = xpad[slot_tok]
    y = _grouped_experts(xs, blk_e, w1, w3, w2)
    return y, dest


def _post_kernel(x1_ref, sh_ref, yg_ref, gt_ref, p_ref, g_ref, b_ref, wp_ref, ng_ref, wg_ref, o_ref, *, alpha):
    gates = gt_ref[...]
    routed = yg_ref[0] * gates[:, 0:1]
    for k in range(1, TOP_K):
        routed = routed + yg_ref[k] * gates[:, k:k + 1]
    y = alpha * x1_ref[...] + (sh_ref[...] + routed)
    x2 = _layer_norm(y, g_ref[...], b_ref[...])
    e = _dot(p_ref[...].astype(MXU_DTYPE), wp_ref[...])
    e = e * lax.rsqrt(jnp.mean(e * e, axis=-1, keepdims=True) + RMS_EPS) * ng_ref[...]
    gate = _sigmoid(_dot(x2.astype(MXU_DTYPE), wg_ref[...]))
    o_ref[...] = x2 + gate * e


def _post_block(x1, shared, y_pairs, gates_t, p, ln_g, ln_b, w_proj, norm_g, w_gate, alpha):
    t, d = x1.shape
    pd = p.shape[1]
    tm = _pick_tm(t, 256)
    row = lambda n: pl.BlockSpec((tm, n), lambda i: (i, 0))
    res = lambda a, c: pl.BlockSpec((a, c), lambda i: (0, 0))
    return pl.pallas_call(
        functools.partial(_post_kernel, alpha=alpha),
        out_shape=jax.ShapeDtypeStruct((t, d), jnp.float32),
        grid=(t // tm,),
        in_specs=[row(d), row(d), pl.BlockSpec((TOP_K, tm, d), lambda i: (0, i, 0)), row(TOP_K), row(pd),
                  res(1, d), res(1, d), res(pd, d), res(1, d), res(d, d)],
        out_specs=row(d),
        compiler_params=_cparams("parallel"),
        name="moe_combine_ln_ple",
    )(x1, shared, y_pairs, gates_t, p, ln_g.reshape(1, d), ln_b.reshape(1, d), w_proj.astype(MXU_DTYPE),
      norm_g.reshape(1, d), w_gate.astype(MXU_DTYPE))


def _rope_tables(s_len):
    inv = 1.0 / (ROPE_THETA ** (np.arange(0, HEAD_DIM, 2, dtype=np.float32) / HEAD_DIM))
    ang = jnp.arange(s_len, dtype=jnp.float32)[:, None] * jnp.asarray(inv, dtype=jnp.float32)[None, :]
    return jnp.cos(ang), jnp.sin(ang)


def _rope(x, cos, sin):
    b, l, w = x.shape
    xh = x.reshape(b, l, w // HEAD_DIM, HEAD_DIM)
    half = HEAD_DIM // 2
    x1, x2 = xh[..., :half], xh[..., half:]
    c = cos[:, None, :]
    s = sin[:, None, :]
    return jnp.concatenate([x1 * c - x2 * s, x2 * c + x1 * s], axis=-1).reshape(b, l, w)


def _split_cols(h, sizes):
    out, c = [], 0
    for n in sizes:
        out.append(h[..., c:c + n])
        c += n
    return out


def _even_mixer(x, w_in, b_gate, pos_k, w1_k, w2_k, pos_v, w1_v, w2_v, cos, sin):
    b, s, d = x.shape
    mix = A_HEADS * HEAD_DIM
    splits = (mix, HEAD_DIM, HEAD_DIM, IDX_HEADS * IDX_DIM, IDX_DIM, IDX_HEADS, B_HEADS * HEAD_DIM,
              6 * HEAD_DIM, 3 * B_HEADS)
    h = _matmul(x.reshape(b * s, d), w_in, name="even_in_proj").reshape(b, s, -1)
    q_a, k_a, v_a, q_i, k_i, w_i, q_b, kv_b, g_b = _split_cols(h, splits)
    o_a = _dsa_attention(_rope(q_a, cos, sin), _rope(k_a, cos, sin), v_a,
                         _rope(q_i, cos, sin), _rope(k_i, cos, sin), w_i)
    k_c, v_c, k_s, v_s, k_w, v_w = _split_cols(kv_b, (HEAD_DIM,) * 6)
    nc = s // CMP_STRIDE
    c_last = jnp.minimum(jnp.arange(nc) * CMP_STRIDE + CMP_LEN - 1, s - 1)
    kc = _rope(_compress(k_c, pos_k, w1_k, w2_k), cos[c_last], sin[c_last])
    vc = _compress(v_c, pos_v, w1_v, w2_v)
    gate = (g_b + b_gate).reshape(b, s, B_HEADS, 3)
    o_b = _nsa_attention(_rope(q_b, cos, sin), kc, vc, _rope(k_s, cos, sin), v_s,
                         _rope(k_w, cos, sin), v_w, gate)
    return jnp.concatenate([o_a, o_b], axis=-1).reshape(b * s, 2 * mix)


def _odd_mixer(x, w_in, b_f):
    b, s, d = x.shape
    mix = C_HEADS * HEAD_DIM
    h = _matmul(x.reshape(b * s, d), w_in, name="odd_in_proj").reshape(b, s, -1)
    q, k, v, f = _split_cols(h, (mix, mix, mix, C_HEADS))
    log_f = jax.nn.log_sigmoid((f + b_f).astype(jnp.float32))
    return _fox_attention(q, k, v, log_f).reshape(b * s, mix)


def kernel(x, p, ev_w_in, ev_b_gate, ev_cmp_pos_k, ev_cmp_w1_k, ev_cmp_w2_k, ev_cmp_pos_v, ev_cmp_w1_v, ev_cmp_w2_v, ev_w_o, od_w_in, od_b_f, od_w_o, ln1_g, ln1_b, ln2_g, ln2_b, router_w, router_bias, exp_w1, exp_w3, exp_w2, sh_w1, sh_w3, sh_w2, ple_w_gate, ple_w_proj, ple_norm_g):
    b, s, d = x.shape
    depth = p.shape[0]
    alpha = float((2.0 * depth) ** 0.25)
    cos, sin = _rope_tables(s)
    xf = x.reshape(b * s, d)
    for i in range(depth):
        j = i // 2
        x3 = xf.reshape(b, s, d)
        if i % 2 == 0:
            o = _even_mixer(x3, ev_w_in[j], ev_b_gate[j], ev_cmp_pos_k[j], ev_cmp_w1_k[j], ev_cmp_w2_k[j],
                            ev_cmp_pos_v[j], ev_cmp_w1_v[j], ev_cmp_w2_v[j], cos, sin)
            w_o = ev_w_o[j]
        else:
            o = _odd_mixer(x3, od_w_in[j], od_b_f[j])
            w_o = od_w_o[j]
        x1, shared, logits_t = _mid_block(o, xf, w_o, ln1_g[i], ln1_b[i], sh_w1[i], sh_w3[i], sh_w2[i],
                                          router_w[i], alpha)
        eidx, gates, rank, counts = _route(logits_t, router_bias[i])
        y, dest = _moe_routed(x1, eidx, rank, counts, exp_w1[i], exp_w3[i], exp_w2[i])
        xf = _post_block(x1, shared, y[dest], gates.T, p[i].reshape(b * s, -1), ln2_g[i], ln2_b[i],
                         ple_w_proj[i], ple_norm_g[i], ple_w_gate[i], alpha)
    return xf.reshape(b, s, d)
```

```python
import functools

import numpy as np
import jax
import jax.numpy as jnp
from jax import lax
from jax.experimental import pallas as pl
from jax.experimental.pallas import tpu as pltpu

HEAD_DIM = 64
QBLOCK = 128
ROPE_THETA = 10000.0
LN_EPS = 1e-5
RMS_EPS = 1e-6
NEG = -1e30
BIG = 1e30

A_HEADS = 8
IDX_HEADS = 8
IDX_DIM = 64
DSA_TOPK_MAX = 256

B_HEADS = 8
CMP_LEN = 32
CMP_STRIDE = 16
SEL_LEN = 32
SEL_BLOCKS_MAX = 8
WINDOW = 512

C_HEADS = 16

N_EXPERTS = 64
TOP_K = 8
N_GROUPS = 8
TOPK_GROUPS = 4
ROUTED_SCALE = 2.5
EXPERT_BLOCK = 256

LANES = 128
SUBLANES = 8
VMEM_LIMIT = 48 * 1024 * 1024

MXU_DTYPE = jnp.bfloat16
KEY_TILE = 256
ATTN_SCALE = float(HEAD_DIM ** -0.5 * np.log2(np.e))
LOG2E = float(np.log2(np.e))
INT_MIN = -(2 ** 31)
IDX_ALL = 2 ** 30


def _order_key(bits):
    return bits ^ ((bits >> 31) & 0x7FFFFFFF)


NEG_KEY = int(_order_key(np.float32(NEG).view(np.int32).astype(np.int64)).astype(np.int32))


def _cparams(*sem):
    return pltpu.CompilerParams(dimension_semantics=sem, vmem_limit_bytes=VMEM_LIMIT)


def _dot(a, b):
    return jnp.dot(a, b, preferred_element_type=jnp.float32)


def _dot_nt(a, b):
    return lax.dot_general(a, b, (((1,), (1,)), ((), ())), preferred_element_type=jnp.float32)


def _silu(x):
    return x * (1.0 / (1.0 + jnp.exp(-x)))


def _sigmoid(x):
    return 1.0 / (1.0 + jnp.exp(-x))


def _mm_kernel(x_ref, w_ref, o_ref):
    o_ref[...] = _dot(x_ref[...].astype(MXU_DTYPE), w_ref[...])


def _pick_tm(t, cap):
    tm = min(cap, t)
    while t % tm:
        tm //= 2
    return tm


def _matmul(x, w, *, tm_cap=512, name="proj"):
    t, k = x.shape
    n = w.shape[1]
    n_pad = -(-n // LANES) * LANES
    w = jnp.pad(w, ((0, 0), (0, n_pad - n))).astype(MXU_DTYPE)
    tm = _pick_tm(t, tm_cap)
    out = pl.pallas_call(
        _mm_kernel,
        out_shape=jax.ShapeDtypeStruct((t, n_pad), jnp.float32),
        grid=(t // tm,),
        in_specs=[pl.BlockSpec((tm, k), lambda i: (i, 0)),
                  pl.BlockSpec((k, n_pad), lambda i: (0, 0))],
        out_specs=pl.BlockSpec((tm, n_pad), lambda i: (i, 0)),
        compiler_params=_cparams("parallel"),
        name=name,
    )(x, w)
    return out


def _online_update(s, drop, vt, m_ref, l_ref, acc_ref, idx=Ellipsis):
    if drop is not None:
        s = s + drop
    m_old = m_ref[idx]
    m_new = jnp.maximum(m_old, jnp.max(s, axis=0, keepdims=True))
    p = jnp.exp2(s - m_new)
    alpha = jnp.exp2(m_old - m_new)
    l_ref[idx] = alpha * l_ref[idx] + jnp.sum(p, axis=0, keepdims=True)
    acc_ref[idx] = alpha * acc_ref[idx] + _dot(vt, p.astype(vt.dtype))
    m_ref[idx] = m_new


def _drop(pred, reps=1):
    d = jnp.where(pred, 0.0, NEG)
    return d if reps == 1 else jnp.concatenate([d] * reps, axis=1)


def _init_state(m_ref, l_ref, acc_ref):
    m_ref[...] = jnp.full(m_ref.shape, NEG, jnp.float32)
    l_ref[...] = jnp.zeros(l_ref.shape, jnp.float32)
    acc_ref[...] = jnp.zeros(acc_ref.shape, jnp.float32)


def _normalized(l_ref, acc_ref, idx=Ellipsis):
    return acc_ref[idx] / l_ref[idx]


def _pipelined_tiles(lo, hi, produce, consume, consume_last=None):
    consume_last = consume_last or consume
    produce(lo, 0)

    def pair(j, carry):
        kt = lo + 2 * j
        produce(kt + 1, 1)
        consume(kt, 0)
        produce(kt + 2, 0)
        consume(kt + 1, 1)
        return carry

    n_body = hi - lo - 1
    lax.fori_loop(0, n_body // 2, pair, 0)

    @pl.when(n_body % 2 == 0)
    def _():
        consume_last(hi - 1, 0)

    @pl.when(n_body % 2 == 1)
    def _():
        produce(hi - 1, 1)
        consume(hi - 2, 0)
        consume_last(hi - 1, 1)


def _store_heads(o_t, o_ref, n_heads, tq):
    for j in range(n_heads // 2):
        pair = jnp.concatenate([o_t[:, (2 * j) * tq:(2 * j + 1) * tq],
                                o_t[:, (2 * j + 1) * tq:(2 * j + 2) * tq]], axis=0)
        o_ref[0, :, 2 * j * HEAD_DIM:(2 * j + 2) * HEAD_DIM] = pair.T.astype(o_ref.dtype)


def _dsa_kernel(qi_ref, w_ref, ki_ref, qa_ref, ka_ref, vat_ref, o_ref,
                key_ref, m_ref, l_ref, acc_ref, sbuf_ref, *, k_sel, idx_scale):
    i = pl.program_id(1)
    tq = QBLOCK
    tk = KEY_TILE
    n_tiles = (i * tq + tq + tk - 1) // tk
    q0 = i * tq
    t_row = q0 + lax.broadcasted_iota(jnp.int32, (1, tq), 1)
    kk = lax.broadcasted_iota(jnp.int32, (tk, tq), 0)

    qi = qi_ref[0].reshape(IDX_HEADS * tq, IDX_DIM)
    w_row = w_ref[0, 0]

    def tile_start(kt):
        return pl.multiple_of(kt * tk, tk)

    def logits_tile(kt, slot):
        sbuf_ref[slot] = _dot_nt(ki_ref[0, pl.ds(tile_start(kt), tk), :], qi)

    def score_tile(kt, slot):
        k0 = tile_start(kt)
        z = jnp.maximum(sbuf_ref[slot], 0.0) * w_row
        sc = z[:, 0:tq]
        for h in range(1, IDX_HEADS):
            sc = sc + z[:, h * tq:(h + 1) * tq]
        sc = sc * idx_scale
        sc = jnp.where(k0 + kk <= t_row, sc, NEG)
        bits = lax.bitcast_convert_type(sc, jnp.int32)
        key_ref[pl.ds(k0, tk), :] = _order_key(bits)

    _pipelined_tiles(0, n_tiles, logits_tile, score_tile)

    @pl.when(n_tiles % 2 == 1)
    def _():
        key_ref[pl.ds(tile_start(n_tiles), tk), :] = jnp.full((tk, tq), NEG_KEY, jnp.int32)

    ct = 2 * tk
    kk2 = lax.broadcasted_iota(jnp.int32, (ct, tq), 0)

    def count(pred_fn):
        def body(kt, acc):
            k0 = pl.multiple_of(kt * ct, ct)
            c = pred_fn(key_ref[pl.ds(k0, ct), :], k0 + kk2).astype(jnp.int32)
            return acc + jnp.sum(c.reshape(ct // SUBLANES, SUBLANES, tq), axis=0)
        acc = lax.fori_loop(0, (n_tiles + 1) // 2, body, jnp.zeros((SUBLANES, tq), jnp.int32))
        return jnp.sum(acc, axis=0, keepdims=True)

    def search(_):
        c0 = count(lambda key, idx: key >= 0)
        thr = jnp.where(c0 >= k_sel, 0, INT_MIN).astype(jnp.int32)

        def bit_step(b, thr):
            trial = thr | (jnp.int32(1) << (30 - b))
            c = count(lambda key, idx: key >= trial)
            return jnp.where(c >= k_sel, trial, thr)

        thr = lax.fori_loop(0, 31, bit_step, thr)
        c_ge = count(lambda key, idx: key >= thr)

        def tie_search(_):
            c_gt = count(lambda key, idx: key > thr)
            need = k_sel - c_gt

            def idx_step(b, u):
                trial = u | (jnp.int32(1) << (20 - b))
                c = count(lambda key, idx: (key == thr) & (idx < trial))
                return jnp.where(c < need, trial, u)

            return lax.fori_loop(0, 21, idx_step, jnp.zeros((1, tq), jnp.int32))

        cut = lax.cond(jnp.max(c_ge) > k_sel, tie_search,
                       lambda _: jnp.full((1, tq), IDX_ALL, jnp.int32), 0)
        return thr, cut

    thr, cut = lax.cond(q0 + tq > k_sel, search,
                        lambda _: (jnp.full((1, tq), INT_MIN, jnp.int32),
                                   jnp.full((1, tq), IDX_ALL, jnp.int32)), 0)

    qa = qa_ref[0].reshape(A_HEADS * tq, HEAD_DIM)
    _init_state(m_ref, l_ref, acc_ref)

    def qk_tile(kt, slot):
        sbuf_ref[slot] = _dot_nt(ka_ref[0, pl.ds(tile_start(kt), tk), :], qa)

    def attn_tile(kt, slot):
        k0 = tile_start(kt)
        key = key_ref[pl.ds(k0, tk), :]
        idx = k0 + kk
        sel = ((key > thr) | ((key == thr) & (idx <= cut))) & (idx <= t_row)
        _online_update(sbuf_ref[slot], _drop(sel, A_HEADS), vat_ref[0, kt], m_ref, l_ref, acc_ref)

    _pipelined_tiles(0, n_tiles, qk_tile, attn_tile)
    _store_heads(_normalized(l_ref, acc_ref), o_ref, A_HEADS, tq)


def _heads_major(x, n_heads, scale=1.0):
    b, s, _ = x.shape
    return jnp.transpose((x * scale).reshape(b, s, n_heads, HEAD_DIM), (0, 2, 1, 3)).astype(MXU_DTYPE)


def _value_tiles_t(v, tk):
    b, s, d = v.shape
    return jnp.transpose(v.reshape(b, s // tk, tk, d), (0, 1, 3, 2)).astype(MXU_DTYPE)


def _per_query_rows(x, tq):
    b, s, h = x.shape
    return jnp.transpose(x.reshape(b, s // tq, tq, h), (0, 1, 3, 2)).reshape(b, s // tq, 1, h * tq)


def _dsa_attention(q_a, k_a, v_a, q_i, k_i, w_i):
    b, s, _ = q_a.shape
    tq = QBLOCK
    nq = s // tq
    k_sel = min(DSA_TOPK_MAX, s // 4)
    tk = KEY_TILE
    assert (s // tk) % 2 == 0
    kern = functools.partial(_dsa_kernel, k_sel=k_sel, idx_scale=float((IDX_HEADS * IDX_DIM) ** -0.5))
    hq = A_HEADS * tq
    return pl.pallas_call(
        kern,
        out_shape=jax.ShapeDtypeStruct((b, s, A_HEADS * HEAD_DIM), MXU_DTYPE),
        grid=(b, nq),
        in_specs=[
            pl.BlockSpec((1, IDX_HEADS, tq, IDX_DIM), lambda bi, i: (bi, 0, i, 0)),
            pl.BlockSpec((1, 1, 1, hq), lambda bi, i: (bi, i, 0, 0)),
            pl.BlockSpec((1, s, IDX_DIM), lambda bi, i: (bi, 0, 0)),
            pl.BlockSpec((1, A_HEADS, tq, HEAD_DIM), lambda bi, i: (bi, 0, i, 0)),
            pl.BlockSpec((1, s, HEAD_DIM), lambda bi, i: (bi, 0, 0)),
            pl.BlockSpec((1, s // tk, HEAD_DIM, tk), lambda bi, i: (bi, 0, 0, 0)),
        ],
        out_specs=pl.BlockSpec((1, tq, A_HEADS * HEAD_DIM), lambda bi, i: (bi, i, 0)),
        scratch_shapes=[pltpu.VMEM((s, tq), jnp.int32),
                        pltpu.VMEM((1, hq), jnp.float32),
                        pltpu.VMEM((1, hq), jnp.float32),
                        pltpu.VMEM((HEAD_DIM, hq), jnp.float32),
                        pltpu.VMEM((2, tk, hq), jnp.float32)],
        compiler_params=_cparams("parallel", "arbitrary"),
        name="dsa_attention",
    )(_heads_major(q_i, IDX_HEADS), _per_query_rows(w_i.astype(jnp.float32), tq),
      k_i.astype(MXU_DTYPE), _heads_major(q_a, A_HEADS, ATTN_SCALE), k_a.astype(MXU_DTYPE),
      _value_tiles_t(v_a, tk))


def _cmp_kernel(ck_ref, pe_ref, w1_ref, w2_ref, o_ref):
    nc = ck_ref.shape[1]
    half = ck_ref.shape[2]
    ck = ck_ref[0]
    a = _dot((ck + pe_ref[0:1, :]).astype(MXU_DTYPE), w1_ref[0:half, :])
    bm = _dot((ck + pe_ref[1:2, :]).astype(MXU_DTYPE), w1_ref[half:2 * half, :])
    h = a + pltpu.roll(bm, nc - 1, 0)
    o_ref[0] = _dot(_silu(h).astype(MXU_DTYPE), w2_ref[...])


def _compress(kv, pe, w1, w2):
    b, s, d = kv.shape
    nc = s // CMP_STRIDE
    half = CMP_STRIDE * d
    ck = kv.reshape(b, nc, half)
    pe2 = pe.reshape(2, half)
    return pl.pallas_call(
        _cmp_kernel,
        out_shape=jax.ShapeDtypeStruct((b, nc, d), jnp.float32),
        grid=(b,),
        in_specs=[pl.BlockSpec((1, nc, half), lambda bi: (bi, 0, 0)),
                  pl.BlockSpec((2, half), lambda bi: (0, 0)),
                  pl.BlockSpec((2 * half, d), lambda bi: (0, 0)),
                  pl.BlockSpec((d, d), lambda bi: (0, 0))],
        out_specs=pl.BlockSpec((1, nc, d), lambda bi: (bi, 0, 0)),
        compiler_params=_cparams("parallel"),
        name="nsa_compress",
    )(ck, pe2, w1.astype(MXU_DTYPE), w2.astype(MXU_DTYPE))


def _split3(x):
    def top(v):
        return lax.bitcast_convert_type(lax.bitcast_convert_type(v, jnp.int32) & jnp.int32(-65536), jnp.float32)
    hi = top(x)
    r1 = x - hi
    mid = top(r1)
    lo = r1 - mid
    return hi.astype(MXU_DTYPE), mid.astype(MXU_DTYPE), lo.astype(MXU_DTYPE)


def _nsa_kernel(q_ref, g_ref, kc_ref, vct_ref, ovt_ref, exp_ref, ks_ref, vst_ref, kw_ref, vwt_ref, o_ref,
                tok_ref, m_ref, l_ref, acc_ref, m2_ref, l2_ref, acc2_ref, out_ref, sbuf_ref, *, n_sel, n_cmp):
    i = pl.program_id(1)
    tq = QBLOCK
    tk = KEY_TILE
    nh = B_HEADS
    hq = nh * tq
    q0 = i * tq
    n_tiles = (q0 + tq + tk - 1) // tk
    t_row = q0 + lax.broadcasted_iota(jnp.int32, (1, tq), 1)
    kk = lax.broadcasted_iota(jnp.int32, (tk, tq), 0)
    q = q_ref[0].reshape(hq, HEAD_DIM)
    gates = g_ref[0, 0]

    nc = kc_ref.shape[1]
    c_id = lax.broadcasted_iota(jnp.int32, (nc, tq), 0)
    valid_c = (c_id * CMP_STRIDE + (CMP_LEN - 1) <= t_row) & (c_id < n_cmp)
    keep_c = jnp.concatenate([jnp.where(valid_c, 1.0, 0.0)] * nh, axis=1)
    s_c = _dot_nt(kc_ref[0], q) + _drop(valid_c, nh)
    e_c = jnp.exp2(s_c - jnp.max(s_c, axis=0, keepdims=True)) * keep_c
    den = jnp.sum(e_c, axis=0, keepdims=True)
    p_c = e_c / jnp.where(den > 0.0, den, 1.0)
    out_ref[...] = gates[0:1, :] * _dot(vct_ref[0], p_c.astype(MXU_DTYPE))

    p_sum = p_c[:, 0:tq]
    for h in range(1, nh):
        p_sum = p_sum + p_c[:, h * tq:(h + 1) * tq]
    ovt = ovt_ref[...]
    pieces = _split3(p_sum)
    imp = _dot(ovt, pieces[0]) + _dot(ovt, pieces[1]) + _dot(ovt, pieces[2])
    n_sb = ovt.shape[0]
    j_id = lax.broadcasted_iota(jnp.int32, (n_sb, tq), 0)
    cur = t_row >> (SEL_LEN.bit_length() - 1)
    forced = (j_id == 0) | (j_id == cur) | (j_id == cur - 1)
    future = j_id * SEL_LEN > t_row
    imp = jnp.where(forced, BIG, jnp.where(future, NEG, imp))
    chosen = jnp.zeros((n_sb, tq), jnp.bool_)
    for _ in range(n_sel):
        cand = jnp.where(chosen, -jnp.inf, imp)
        best = jnp.max(cand, axis=0, keepdims=True)
        first = jnp.min(jnp.where(cand == best, j_id, n_sb), axis=0, keepdims=True)
        chosen = chosen | (j_id == first)
    tok_ref[...] = _dot(exp_ref[...], jnp.where(chosen, 1.0, 0.0).astype(MXU_DTYPE))

    _init_state(m_ref, l_ref, acc_ref)
    _init_state(m2_ref, l2_ref, acc2_ref)
    win_lo = jnp.maximum(q0 - WINDOW, 0) // tk

    def tile_start(kt):
        return pl.multiple_of(kt * tk, tk)

    def sel_qk(kt, slot):
        sbuf_ref[slot] = _dot_nt(ks_ref[0, pl.ds(tile_start(kt), tk), :], q)

    def sel_tile(kt, slot):
        k0 = tile_start(kt)
        sel = (tok_ref[pl.ds(k0, tk), :] > 0.5) & (k0 + kk <= t_row)
        _online_update(sbuf_ref[slot], _drop(sel, nh), vst_ref[0, kt], m_ref, l_ref, acc_ref)

    def win_qk(kt, slot):
        sbuf_ref[slot] = _dot_nt(kw_ref[0, pl.ds(tile_start(kt), tk), :], q)

    def win_tile(kt, slot):
        kpos = tile_start(kt) + kk
        ok = (kpos <= t_row) & (kpos > t_row - WINDOW)
        _online_update(sbuf_ref[slot], _drop(ok, nh), vwt_ref[0, kt], m2_ref, l2_ref, acc2_ref)

    _pipelined_tiles(0, n_tiles, sel_qk, sel_tile)
    _pipelined_tiles(win_lo, n_tiles, win_qk, win_tile)
    o_t = out_ref[...] + gates[1:2, :] * _normalized(l_ref, acc_ref) + gates[2:3, :] * _normalized(l2_ref, acc2_ref)
    _store_heads(o_t, o_ref, nh, tq)


def _nsa_attention(q_b, k_c, v_c, k_s, v_s, k_w, v_w, gate):
    b, s, _ = q_b.shape
    tq = QBLOCK
    nq = s // tq
    nh = B_HEADS
    hq = nh * tq
    nc = k_c.shape[1]
    n_cmp = (s - CMP_LEN) // CMP_STRIDE + 1
    n_sb = s // SEL_LEN
    n_sel = min(SEL_BLOCKS_MAX, n_sb)
    c_start = np.arange(nc) * CMP_STRIDE
    s_start = np.arange(n_sb) * SEL_LEN
    ov = np.clip(np.minimum(c_start[:, None] + CMP_LEN, s_start[None, :] + SEL_LEN)
                 - np.maximum(c_start[:, None], s_start[None, :]), 0, None) / CMP_LEN
    ov[n_cmp:] = 0.0
    ovt = jnp.asarray(ov.T, dtype=MXU_DTYPE)
    expand = jnp.asarray((np.arange(s)[:, None] // SEL_LEN == np.arange(n_sb)[None, :]), dtype=MXU_DTYPE)
    g = _sigmoid(gate.astype(jnp.float32))
    g = jnp.transpose(g.reshape(b, nq, tq, nh, 3), (0, 1, 4, 3, 2)).reshape(b, nq, 3, hq)
    tk = KEY_TILE
    kern = functools.partial(_nsa_kernel, n_sel=n_sel, n_cmp=n_cmp)
    full = lambda shape: pl.BlockSpec(shape, lambda bi, i: (0,) * len(shape))
    per_b = lambda shape: pl.BlockSpec(shape, lambda bi, i: (bi,) + (0,) * (len(shape) - 1))
    state = [pltpu.VMEM((1, hq), jnp.float32), pltpu.VMEM((1, hq), jnp.float32),
             pltpu.VMEM((HEAD_DIM, hq), jnp.float32)]
    return pl.pallas_call(
        kern,
        out_shape=jax.ShapeDtypeStruct((b, s, nh * HEAD_DIM), MXU_DTYPE),
        grid=(b, nq),
        in_specs=[
            pl.BlockSpec((1, nh, tq, HEAD_DIM), lambda bi, i: (bi, 0, i, 0)),
            pl.BlockSpec((1, 1, 3, hq), lambda bi, i: (bi, i, 0, 0)),
            per_b((1, nc, HEAD_DIM)),
            per_b((1, HEAD_DIM, nc)),
            full((n_sb, nc)),
            full((s, n_sb)),
            per_b((1, s, HEAD_DIM)),
            per_b((1, s // tk, HEAD_DIM, tk)),
            per_b((1, s, HEAD_DIM)),
            per_b((1, s // tk, HEAD_DIM, tk)),
        ],
        out_specs=pl.BlockSpec((1, tq, nh * HEAD_DIM), lambda bi, i: (bi, i, 0)),
        scratch_shapes=[pltpu.VMEM((s, tq), jnp.float32)] + state + state
                       + [pltpu.VMEM((HEAD_DIM, hq), jnp.float32), pltpu.VMEM((2, tk, hq), jnp.float32)],
        compiler_params=_cparams("parallel", "arbitrary"),
        name="nsa_attention",
    )(_heads_major(q_b, nh, ATTN_SCALE), g, k_c.astype(MXU_DTYPE),
      jnp.transpose(v_c, (0, 2, 1)).astype(MXU_DTYPE), ovt, expand,
      k_s.astype(MXU_DTYPE), _value_tiles_t(v_s, tk), k_w.astype(MXU_DTYPE), _value_tiles_t(v_w, tk))


FOX_TQ = 256
FOX_HB = 4
FOX_KPAD = 128


def _fox_kernel(q_ref, k_ref, vt_ref, o_ref, m_ref, l_ref, acc_ref, sbuf_ref):
    i = pl.program_id(2)
    tq = FOX_TQ
    _init_state(m_ref, l_ref, acc_ref)

    def qk_tile(kt, slot):
        k0 = pl.multiple_of(kt * tq, tq)
        for hh in range(FOX_HB):
            sbuf_ref[slot, hh] = _dot_nt(k_ref[0, hh, pl.ds(k0, tq), :], q_ref[0, hh])

    def update(kt, slot, drop):
        for hh in range(FOX_HB):
            _online_update(sbuf_ref[slot, hh], drop, vt_ref[0, hh, kt], m_ref, l_ref, acc_ref, idx=hh)

    def diag_tile(kt, slot):
        kk = lax.broadcasted_iota(jnp.int32, (tq, tq), 0)
        qq = lax.broadcasted_iota(jnp.int32, (tq, tq), 1)
        update(kt, slot, _drop(kk <= qq))

    _pipelined_tiles(0, i + 1, qk_tile, lambda kt, slot: update(kt, slot, None), diag_tile)
    for j in range(FOX_HB // 2):
        pair = jnp.concatenate([_normalized(l_ref, acc_ref, 2 * j), _normalized(l_ref, acc_ref, 2 * j + 1)], axis=0)
        o_ref[0, :, 2 * j * HEAD_DIM:(2 * j + 2) * HEAD_DIM] = pair.T.astype(o_ref.dtype)


def _fox_attention(q, k, v, log_f):
    b, s, _ = q.shape
    nh = C_HEADS
    tq = min(FOX_TQ, s)
    assert tq == FOX_TQ
    d_cum = jnp.cumsum(log_f, axis=1)
    d3 = jnp.stack(_split3(d_cum * LOG2E), axis=-1)
    ones = jnp.ones((b, s, nh, 3), MXU_DTYPE)
    pad = jnp.zeros((b, s, nh, FOX_KPAD - HEAD_DIM - 3), MXU_DTYPE)
    qh = (q * ATTN_SCALE).reshape(b, s, nh, HEAD_DIM).astype(MXU_DTYPE)
    kh = k.reshape(b, s, nh, HEAD_DIM).astype(MXU_DTYPE)
    q_aug = jnp.transpose(jnp.concatenate([qh, ones, pad], axis=-1), (0, 2, 1, 3))
    k_aug = jnp.transpose(jnp.concatenate([kh, -d3, pad], axis=-1), (0, 2, 1, 3))
    vt = jnp.transpose(v.reshape(b, s // tq, tq, nh, HEAD_DIM), (0, 3, 1, 4, 2)).astype(MXU_DTYPE)
    return pl.pallas_call(
        _fox_kernel,
        out_shape=jax.ShapeDtypeStruct((b, s, nh * HEAD_DIM), MXU_DTYPE),
        grid=(b, nh // FOX_HB, s // tq),
        in_specs=[
            pl.BlockSpec((1, FOX_HB, tq, FOX_KPAD), lambda bi, j, i: (bi, j, i, 0)),
            pl.BlockSpec((1, FOX_HB, s, FOX_KPAD), lambda bi, j, i: (bi, j, 0, 0)),
            pl.BlockSpec((1, FOX_HB, s // tq, HEAD_DIM, tq), lambda bi, j, i: (bi, j, 0, 0, 0)),
        ],
        out_specs=pl.BlockSpec((1, tq, FOX_HB * HEAD_DIM), lambda bi, j, i: (bi, i, j)),
        scratch_shapes=[pltpu.VMEM((FOX_HB, 1, tq), jnp.float32),
                        pltpu.VMEM((FOX_HB, 1, tq), jnp.float32),
                        pltpu.VMEM((FOX_HB, HEAD_DIM, tq), jnp.float32),
                        pltpu.VMEM((2, FOX_HB, tq, tq), jnp.float32)],
        compiler_params=_cparams("parallel", "parallel", "arbitrary"),
        name="fox_attention",
    )(q_aug, k_aug, vt)


def _layer_norm(y, g, b):
    mu = jnp.mean(y, axis=-1, keepdims=True)
    d = y - mu
    var = jnp.mean(d * d, axis=-1, keepdims=True)
    return d * lax.rsqrt(var + LN_EPS) * g + b


def _mid_kernel(o_ref, x_ref, wo_ref, g_ref, b_ref, sw1_ref, sw3_ref, sw2_ref, rw_ref,
                x1_ref, x1b_ref, sh_ref, lg_ref, *, alpha):
    y = alpha * x_ref[...] + _dot(o_ref[...], wo_ref[...])
    x1 = _layer_norm(y, g_ref[...], b_ref[...])
    x1_ref[...] = x1
    xb = x1.astype(MXU_DTYPE)
    x1b_ref[...] = xb
    hid = _silu(_dot(xb, sw1_ref[...])) * _dot(xb, sw3_ref[...])
    sh_ref[...] = _dot(hid.astype(MXU_DTYPE), sw2_ref[...])
    lg_ref[...] = _dot_nt(rw_ref[...], xb)


def _mid_block(o, x, w_o, ln_g, ln_b, sw1, sw3, sw2, router_w, alpha):
    t, d = x.shape
    f = sw1.shape[1]
    e = router_w.shape[1]
    tm = _pick_tm(t, 512)
    row = lambda n: pl.BlockSpec((tm, n), lambda i: (i, 0))
    res = lambda a, c: pl.BlockSpec((a, c), lambda i: (0, 0))
    return pl.pallas_call(
        functools.partial(_mid_kernel, alpha=alpha),
        out_shape=(jax.ShapeDtypeStruct((t, d), jnp.float32),
                   jax.ShapeDtypeStruct((t, d), MXU_DTYPE),
                   jax.ShapeDtypeStruct((t, d), jnp.float32),
                   jax.ShapeDtypeStruct((e, t), jnp.float32)),
        grid=(t // tm,),
        in_specs=[row(o.shape[1]), row(d), res(o.shape[1], d), res(1, d), res(1, d),
                  res(d, f), res(d, f), res(f, d), res(e, d)],
        out_specs=(row(d), row(d), row(d), pl.BlockSpec((e, tm), lambda i: (0, i))),
        compiler_params=_cparams("parallel"),
        name="mixer_out_ln_shared_router",
    )(o, x, w_o.astype(MXU_DTYPE), ln_g.reshape(1, d), ln_b.reshape(1, d),
      sw1.astype(MXU_DTYPE), sw3.astype(MXU_DTYPE), sw2.astype(MXU_DTYPE), router_w.T.astype(MXU_DTYPE))


def _expert_kernel(blk_e_ref, xs_ref, w1_ref, w3_ref, w2_ref, y_ref):
    xe = xs_ref[...]
    hid = _silu(_dot(xe, w1_ref[0])) * _dot(xe, w3_ref[0])
    y_ref[...] = _dot(hid.astype(MXU_DTYPE), w2_ref[0]).astype(y_ref.dtype)


def _grouped_experts(xs, blk_e, w1, w3, w2):
    n_slots, d = xs.shape
    f = w1.shape[2]
    n_blocks = n_slots // EXPERT_BLOCK
    return pl.pallas_call(
        _expert_kernel,
        out_shape=jax.ShapeDtypeStruct((n_slots, d), MXU_DTYPE),
        grid_spec=pltpu.PrefetchScalarGridSpec(
            num_scalar_prefetch=1,
            grid=(n_blocks,),
            in_specs=[pl.BlockSpec((EXPERT_BLOCK, d), lambda i, be: (i, 0)),
                      pl.BlockSpec((1, d, f), lambda i, be: (be[i], 0, 0)),
                      pl.BlockSpec((1, d, f), lambda i, be: (be[i], 0, 0)),
                      pl.BlockSpec((1, f, d), lambda i, be: (be[i], 0, 0))],
            out_specs=pl.BlockSpec((EXPERT_BLOCK, d), lambda i, be: (i, 0)),
        ),
        compiler_params=_cparams("arbitrary"),
        name="routed_experts",
    )(blk_e, xs, w1.astype(MXU_DTYPE), w3.astype(MXU_DTYPE), w2.astype(MXU_DTYPE))


ROUTER_TM = 512


def _pick_rows(rows, row_id, n):
    out = jnp.zeros((n,) + rows[0].shape[1:], rows[0].dtype)
    for r in range(n):
        out = jnp.where(row_id == r, rows[r], out)
    return out


def _router_kernel(lg_ref, bias_ref, tri_ref, eidx_ref, gate_ref, rank_ref, cnt_ref, carry_ref):
    e, tm = lg_ref.shape
    per_group = e // N_GROUPS

    @pl.when(pl.program_id(0) == 0)
    def _():
        carry_ref[...] = jnp.zeros(carry_ref.shape, jnp.float32)

    s = _sigmoid(lg_ref[...])
    sb = s + bias_ref[...]
    neg_inf = -jnp.inf

    sub_id = lax.broadcasted_iota(jnp.int32, (per_group, tm), 0)
    g_rows = []
    for g in range(N_GROUPS):
        blk = sb[g * per_group:(g + 1) * per_group, :]
        m1 = jnp.max(blk, axis=0, keepdims=True)
        f1 = jnp.min(jnp.where(blk == m1, sub_id, per_group), axis=0, keepdims=True)
        m2 = jnp.max(jnp.where(sub_id == f1, neg_inf, blk), axis=0, keepdims=True)
        g_rows.append(m1 + m2)
    g_id = lax.broadcasted_iota(jnp.int32, (N_GROUPS, tm), 0)
    gscore = _pick_rows(g_rows, g_id, N_GROUPS)

    e_id = lax.broadcasted_iota(jnp.int32, (e, tm), 0)
    e_group = e_id // per_group if per_group & (per_group - 1) else e_id >> (per_group.bit_length() - 1)
    g_taken = jnp.zeros((N_GROUPS, tm), jnp.bool_)
    e_allowed = jnp.zeros((e, tm), jnp.bool_)
    for _ in range(TOPK_GROUPS):
        cand = jnp.where(g_taken, neg_inf, gscore)
        best = jnp.max(cand, axis=0, keepdims=True)
        first = jnp.min(jnp.where(cand == best, g_id, N_GROUPS), axis=0, keepdims=True)
        g_taken = g_taken | (g_id == first)
        e_allowed = e_allowed | (e_group == first)

    masked = jnp.where(e_allowed, sb, NEG)
    chosen = jnp.zeros((e, tm), jnp.bool_)
    id_rows, sel_rows = [], []
    for _ in range(TOP_K):
        cand = jnp.where(chosen, neg_inf, masked)
        best = jnp.max(cand, axis=0, keepdims=True)
        first = jnp.min(jnp.where(cand == best, e_id, e), axis=0, keepdims=True)
        hit = e_id == first
        chosen = chosen | hit
        id_rows.append(first)
        sel_rows.append(jnp.sum(jnp.where(hit, s, 0.0), axis=0, keepdims=True))
    k_id = lax.broadcasted_iota(jnp.int32, (TOP_K, tm), 0)
    eidx = _pick_rows(id_rows, k_id, TOP_K)
    sel = _pick_rows(sel_rows, k_id, TOP_K)
    eidx_ref[...] = eidx
    gate_ref[...] = sel / jnp.sum(sel, axis=0, keepdims=True) * ROUTED_SCALE

    chosen_f = jnp.where(chosen, 1.0, 0.0)
    incl = _dot(chosen_f.astype(MXU_DTYPE), tri_ref[...])
    rank_dense = carry_ref[...] + incl - chosen_f
    rank_rows = [jnp.sum(jnp.where(e_id == id_rows[r], rank_dense, 0.0), axis=0, keepdims=True)
                 for r in range(TOP_K)]
    rank_ref[...] = _pick_rows(rank_rows, k_id, TOP_K).astype(jnp.int32)
    carry_ref[...] = carry_ref[...] + jnp.sum(chosen_f, axis=1, keepdims=True)
    cnt_ref[...] = carry_ref[...]


def _route(logits_t, router_bias):
    e, t = logits_t.shape
    tm = _pick_tm(t, ROUTER_TM)
    tri = jnp.asarray(np.triu(np.ones((tm, tm), np.float32)), dtype=MXU_DTYPE)
    kt = lambda dt: jax.ShapeDtypeStruct((TOP_K, t), dt)
    col = pl.BlockSpec((TOP_K, tm), lambda i: (0, i))
    eidx, gates, rank, cnt = pl.pallas_call(
        _router_kernel,
        out_shape=(kt(jnp.int32), kt(jnp.float32), kt(jnp.int32), jax.ShapeDtypeStruct((e, 1), jnp.float32)),
        grid=(t // tm,),
        in_specs=[pl.BlockSpec((e, tm), lambda i: (0, i)),
                  pl.BlockSpec((e, 1), lambda i: (0, 0)),
                  pl.BlockSpec((tm, tm), lambda i: (0, 0))],
        out_specs=(col, col, col, pl.BlockSpec((e, 1), lambda i: (0, 0))),
        scratch_shapes=[pltpu.VMEM((e, 1), jnp.float32)],
        compiler_params=_cparams("arbitrary"),
        name="moe_router",
    )(logits_t, router_bias.astype(jnp.float32).reshape(e, 1), tri)
    return eidx, gates, rank, cnt[:, 0].astype(jnp.int32)


def _moe_routed(x1b, eidx, rank, counts, w1, w3, w2):
    n_tok, d = x1b.shape
    tk = n_tok * TOP_K
    n_blocks = (tk + N_EXPERTS * (EXPERT_BLOCK - 1)) // EXPERT_BLOCK + 1
    n_slots = n_blocks * EXPERT_BLOCK
    padded = (counts + EXPERT_BLOCK - 1) // EXPERT_BLOCK * EXPERT_BLOCK
    e_ids = jnp.arange(N_EXPERTS, dtype=jnp.int32)
    pad_end = jnp.sum(jnp.where(e_ids[None, :] <= e_ids[:, None], padded[None, :], 0), axis=1)
    start_pad = (pad_end - padded).astype(jnp.int32)
    start_of = jnp.sum(jnp.where(eidx[None] == e_ids[:, None, None], start_pad[:, None, None], 0), axis=0)
    dest = start_of + rank
    tok = jnp.broadcast_to(jnp.arange(n_tok, dtype=jnp.int32)[None, :], (TOP_K, n_tok))
    slot_tok = jnp.zeros((n_slots,), jnp.int32).at[dest.reshape(tk)].set(
        tok.reshape(tk), unique_indices=True, mode="drop")
    blk_start = jnp.arange(n_blocks, dtype=jnp.int32) * EXPERT_BLOCK
    blk_e = jnp.minimum(jnp.sum((pad_end[None, :] <= blk_start[:, None]).astype(jnp.int32), axis=1),
                        N_EXPERTS - 1).astype(jnp.int32)
    y = _grouped_experts(x1b[slot_tok], blk_e, w1, w3, w2)
    return y, dest


def _post_kernel(x1_ref, sh_ref, yg_ref, gt_ref, p_ref, g_ref, b_ref, wp_ref, ng_ref, wg_ref, o_ref, *, alpha):
    gates = gt_ref[...]
    routed = yg_ref[0].astype(jnp.float32) * gates[:, 0:1]
    for k in range(1, TOP_K):
        routed = routed + yg_ref[k].astype(jnp.float32) * gates[:, k:k + 1]
    y = alpha * x1_ref[...] + (sh_ref[...] + routed)
    x2 = _layer_norm(y, g_ref[...], b_ref[...])
    e = _dot(p_ref[...].astype(MXU_DTYPE), wp_ref[...])
    e = e * lax.rsqrt(jnp.mean(e * e, axis=-1, keepdims=True) + RMS_EPS) * ng_ref[...]
    gate = _sigmoid(_dot(x2.astype(MXU_DTYPE), wg_ref[...]))
    o_ref[...] = x2 + gate * e


def _post_block(x1, shared, y_pairs, gates_t, p, ln_g, ln_b, w_proj, norm_g, w_gate, alpha):
    t, d = x1.shape
    pd = p.shape[1]
    tm = _pick_tm(t, 256)
    row = lambda n: pl.BlockSpec((tm, n), lambda i: (i, 0))
    res = lambda a, c: pl.BlockSpec((a, c), lambda i: (0, 0))
    return pl.pallas_call(
        functools.partial(_post_kernel, alpha=alpha),
        out_shape=jax.ShapeDtypeStruct((t, d), jnp.float32),
        grid=(t // tm,),
        in_specs=[row(d), row(d), pl.BlockSpec((TOP_K, tm, d), lambda i: (0, i, 0)), row(TOP_K), row(pd),
                  res(1, d), res(1, d), res(pd, d), res(1, d), res(d, d)],
        out_specs=row(d),
        compiler_params=_cparams("parallel"),
        name="moe_combine_ln_ple",
    )(x1, shared, y_pairs, gates_t, p, ln_g.reshape(1, d), ln_b.reshape(1, d), w_proj.astype(MXU_DTYPE),
      norm_g.reshape(1, d), w_gate.astype(MXU_DTYPE))


def _rope_tables(s_len):
    inv = 1.0 / (ROPE_THETA ** (np.arange(0, HEAD_DIM, 2, dtype=np.float32) / HEAD_DIM))
    ang = jnp.arange(s_len, dtype=jnp.float32)[:, None] * jnp.asarray(inv, dtype=jnp.float32)[None, :]
    return jnp.cos(ang), jnp.sin(ang)


def _rope(x, cos, sin):
    b, l, w = x.shape
    xh = x.reshape(b, l, w // HEAD_DIM, HEAD_DIM)
    half = HEAD_DIM // 2
    x1, x2 = xh[..., :half], xh[..., half:]
    c = cos[:, None, :]
    s = sin[:, None, :]
    return jnp.concatenate([x1 * c - x2 * s, x2 * c + x1 * s], axis=-1).reshape(b, l, w)


def _split_cols(h, sizes):
    out, c = [], 0
    for n in sizes:
        out.append(h[..., c:c + n])
        c += n
    return out


def _even_mixer(x, w_in, b_gate, pos_k, w1_k, w2_k, pos_v, w1_v, w2_v, cos, sin):
    b, s, d = x.shape
    mix = A_HEADS * HEAD_DIM
    splits = (mix, HEAD_DIM, HEAD_DIM, IDX_HEADS * IDX_DIM, IDX_DIM, IDX_HEADS, B_HEADS * HEAD_DIM,
              6 * HEAD_DIM, 3 * B_HEADS)
    h = _matmul(x.reshape(b * s, d), w_in, name="even_in_proj").reshape(b, s, -1)
    q_a, k_a, v_a, q_i, k_i, w_i, q_b, kv_b, g_b = _split_cols(h, splits)
    o_a = _dsa_attention(_rope(q_a, cos, sin), _rope(k_a, cos, sin), v_a,
                         _rope(q_i, cos, sin), _rope(k_i, cos, sin), w_i)
    k_c, v_c, k_s, v_s, k_w, v_w = _split_cols(kv_b, (HEAD_DIM,) * 6)
    nc = s // CMP_STRIDE
    c_last = jnp.minimum(jnp.arange(nc) * CMP_STRIDE + CMP_LEN - 1, s - 1)
    kc = _rope(_compress(k_c, pos_k, w1_k, w2_k), cos[c_last], sin[c_last])
    vc = _compress(v_c, pos_v, w1_v, w2_v)
    gate = (g_b + b_gate).reshape(b, s, B_HEADS, 3)
    o_b = _nsa_attention(_rope(q_b, cos, sin), kc, vc, _rope(k_s, cos, sin), v_s,
                         _rope(k_w, cos, sin), v_w, gate)
    return jnp.concatenate([o_a, o_b], axis=-1).reshape(b * s, 2 * mix)


def _odd_mixer(x, w_in, b_f):
    b, s, d = x.shape
    mix = C_HEADS * HEAD_DIM
    h = _matmul(x.reshape(b * s, d), w_in, name="odd_in_proj").reshape(b, s, -1)
    q, k, v, f = _split_cols(h, (mix, mix, mix, C_HEADS))
    log_f = jax.nn.log_sigmoid((f + b_f).astype(jnp.float32))
    return _fox_attention(q, k, v, log_f).reshape(b * s, mix)


def kernel(x, p, ev_w_in, ev_b_gate, ev_cmp_pos_k, ev_cmp_w1_k, ev_cmp_w2_k, ev_cmp_pos_v, ev_cmp_w1_v, ev_cmp_w2_v, ev_w_o, od_w_in, od_b_f, od_w_o, ln1_g, ln1_b, ln2_g, ln2_b, router_w, router_bias, exp_w1, exp_w3, exp_w2, sh_w1, sh_w3, sh_w2, ple_w_gate, ple_w_proj, ple_norm_g):
    b, s, d = x.shape
    depth = p.shape[0]
    alpha = float((2.0 * depth) ** 0.25)
    cos, sin = _rope_tables(s)
    xf = x.reshape(b * s, d)
    for i in range(depth):
        j = i // 2
        x3 = xf.reshape(b, s, d)
        if i % 2 == 0:
            o = _even_mixer(x3, ev_w_in[j], ev_b_gate[j], ev_cmp_pos_k[j], ev_cmp_w1_k[j], ev_cmp_w2_k[j],
                            ev_cmp_pos_v[j], ev_cmp_w1_v[j], ev_cmp_w2_v[j], cos, sin)
            w_o = ev_w_o[j]
        else:
            o = _odd_mixer(x3, od_w_in[j], od_b_f[j])
            w_o = od_w_o[j]
        x1, x1b, shared, logits_t = _mid_block(o, xf, w_o, ln1_g[i], ln1_b[i], sh_w1[i], sh_w3[i], sh_w2[i],
                                               router_w[i], alpha)
        eidx, gates, rank, counts = _route(logits_t, router_bias[i])
        y, dest = _moe_routed(x1b, eidx, rank, counts, exp_w1[i], exp_w3[i], exp_w2[i])
        xf = _post_block(x1, shared, y[dest], gates.T, p[i].reshape(b * s, -1), ln2_g[i], ln2_b[i],
                         ple_w_proj[i], ple_norm_g[i], ple_w_gate[i], alpha)
    return xf.reshape(b, s, d)
```

```python
import functools

import numpy as np
import jax
import jax.numpy as jnp
from jax import lax
from jax.experimental import pallas as pl
from jax.experimental.pallas import tpu as pltpu

HEAD_DIM = 64
QBLOCK = 128
ROPE_THETA = 10000.0
LN_EPS = 1e-5
RMS_EPS = 1e-6
NEG = -1e30
BIG = 1e30

A_HEADS = 8
IDX_HEADS = 8
IDX_DIM = 64
DSA_TOPK_MAX = 256

B_HEADS = 8
CMP_LEN = 32
CMP_STRIDE = 16
SEL_LEN = 32
SEL_BLOCKS_MAX = 8
WINDOW = 512

C_HEADS = 16

N_EXPERTS = 64
TOP_K = 8
N_GROUPS = 8
TOPK_GROUPS = 4
ROUTED_SCALE = 2.5
EXPERT_BLOCK = 512

LANES = 128
SUBLANES = 8
VMEM_LIMIT = 48 * 1024 * 1024

MXU_DTYPE = jnp.bfloat16
KEY_TILE = 256
ATTN_SCALE = float(HEAD_DIM ** -0.5 * np.log2(np.e))
LOG2E = float(np.log2(np.e))
INT_MIN = -(2 ** 31)
IDX_ALL = 2 ** 30


def _order_key(bits):
    return bits ^ ((bits >> 31) & 0x7FFFFFFF)


NEG_KEY = int(_order_key(np.float32(NEG).view(np.int32).astype(np.int64)).astype(np.int32))


def _cparams(*sem):
    return pltpu.CompilerParams(dimension_semantics=sem, vmem_limit_bytes=VMEM_LIMIT)


def _dot(a, b):
    return jnp.dot(a, b, preferred_element_type=jnp.float32)


def _dot_nt(a, b):
    return lax.dot_general(a, b, (((1,), (1,)), ((), ())), preferred_element_type=jnp.float32)


def _silu(x):
    return x * (1.0 / (1.0 + jnp.exp(-x)))


def _sigmoid(x):
    return 1.0 / (1.0 + jnp.exp(-x))


def _mm_kernel(x_ref, w_ref, o_ref):
    o_ref[...] = _dot(x_ref[...].astype(MXU_DTYPE), w_ref[...])


def _pick_tm(t, cap):
    tm = min(cap, t)
    while t % tm:
        tm //= 2
    return tm


def _matmul(x, w, *, tm_cap=512, name="proj"):
    t, k = x.shape
    n = w.shape[1]
    n_pad = -(-n // LANES) * LANES
    w = jnp.pad(w, ((0, 0), (0, n_pad - n))).astype(MXU_DTYPE)
    tm = _pick_tm(t, tm_cap)
    out = pl.pallas_call(
        _mm_kernel,
        out_shape=jax.ShapeDtypeStruct((t, n_pad), jnp.float32),
        grid=(t // tm,),
        in_specs=[pl.BlockSpec((tm, k), lambda i: (i, 0)),
                  pl.BlockSpec((k, n_pad), lambda i: (0, 0))],
        out_specs=pl.BlockSpec((tm, n_pad), lambda i: (i, 0)),
        compiler_params=_cparams("parallel"),
        name=name,
    )(x, w)
    return out


def _rope_lanes(x, cos, sin_signed):
    lane = lax.broadcasted_iota(jnp.int32, (x.shape[0], LANES), 1)
    first_half = (lane % HEAD_DIM) < HEAD_DIM // 2
    out = []
    for c in range(x.shape[1] // LANES):
        blk = x[:, c * LANES:(c + 1) * LANES]
        partner = jnp.where(first_half, pltpu.roll(blk, LANES - HEAD_DIM // 2, 1), pltpu.roll(blk, HEAD_DIM // 2, 1))
        out.append(blk * cos + partner * sin_signed)
    return jnp.concatenate(out, axis=1)


def _even_proj_kernel(x_ref, wq_ref, wr_ref, cos_ref, sin_ref, qa_ref, qi_ref, qb_ref, rest_ref):
    xb = x_ref[...].astype(MXU_DTYPE)
    cos = cos_ref[...]
    sin = sin_ref[...]
    for g, (o_ref, scale) in enumerate(((qa_ref, ATTN_SCALE), (qi_ref, 1.0), (qb_ref, ATTN_SCALE))):
        roped = _rope_lanes(_dot(xb, wq_ref[g]), cos, sin)
        o_ref[...] = (roped * scale).astype(o_ref.dtype)
    rest_ref[...] = _dot(xb, wr_ref[...])


def _even_projection(x, w_in, s_len):
    t, d = x.shape
    mix = A_HEADS * HEAD_DIM
    q_cols = [(0, mix), (mix + 2 * HEAD_DIM, IDX_HEADS * IDX_DIM),
              (mix + 2 * HEAD_DIM + IDX_HEADS * IDX_DIM + IDX_DIM + IDX_HEADS, B_HEADS * HEAD_DIM)]
    wq = jnp.stack([w_in[:, c:c + n] for c, n in q_cols]).astype(MXU_DTYPE)
    keep = np.ones(w_in.shape[1], bool)
    for c, n in q_cols:
        keep[c:c + n] = False
    rest_cols = np.nonzero(keep)[0]
    n_rest = -(-len(rest_cols) // LANES) * LANES
    wr = jnp.pad(w_in[:, rest_cols], ((0, 0), (0, n_rest - len(rest_cols)))).astype(MXU_DTYPE)
    inv = 1.0 / (ROPE_THETA ** (np.arange(0, HEAD_DIM, 2, dtype=np.float32) / HEAD_DIM))
    lane = np.arange(LANES)
    ang = jnp.arange(s_len, dtype=jnp.float32)[:, None] * jnp.asarray(inv[lane % (HEAD_DIM // 2)])[None, :]
    sign = np.where((lane % HEAD_DIM) < HEAD_DIM // 2, -1.0, 1.0).astype(np.float32)
    cos_t, sin_t = jnp.cos(ang), jnp.sin(ang) * sign[None, :]
    tm = _pick_tm(s_len, 512)
    per_s = s_len // tm
    row = lambda n: pl.BlockSpec((tm, n), lambda i: (i, 0))
    tab = pl.BlockSpec((tm, LANES), lambda i: (i % per_s, 0))
    qshape = jax.ShapeDtypeStruct((t, mix), MXU_DTYPE)
    qa, qi, qb, rest = pl.pallas_call(
        _even_proj_kernel,
        out_shape=(qshape, qshape, qshape, jax.ShapeDtypeStruct((t, n_rest), jnp.float32)),
        grid=(t // tm,),
        in_specs=[row(d), pl.BlockSpec((3, d, mix), lambda i: (0, 0, 0)), pl.BlockSpec((d, n_rest), lambda i: (0, 0)),
                  tab, tab],
        out_specs=(row(mix), row(mix), row(mix), row(n_rest)),
        compiler_params=_cparams("parallel"),
        name="even_in_proj",
    )(x, wq, wr, cos_t, sin_t)
    return qa, qi, qb, rest[:, :len(rest_cols)]


def _online_update(s, drop, vt, m_ref, l_ref, acc_ref, idx=Ellipsis):
    if drop is not None:
        s = s + drop
    m_old = m_ref[idx]
    m_new = jnp.maximum(m_old, jnp.max(s, axis=0, keepdims=True))
    p = jnp.exp2(s - m_new)
    alpha = jnp.exp2(m_old - m_new)
    l_ref[idx] = alpha * l_ref[idx] + jnp.sum(p, axis=0, keepdims=True)
    acc_ref[idx] = alpha * acc_ref[idx] + _dot(vt, p.astype(vt.dtype))
    m_ref[idx] = m_new


def _drop(pred, reps=1):
    d = jnp.where(pred, 0.0, NEG)
    return d if reps == 1 else jnp.concatenate([d] * reps, axis=1)


def _init_state(m_ref, l_ref, acc_ref):
    m_ref[...] = jnp.full(m_ref.shape, NEG, jnp.float32)
    l_ref[...] = jnp.zeros(l_ref.shape, jnp.float32)
    acc_ref[...] = jnp.zeros(acc_ref.shape, jnp.float32)


def _normalized(l_ref, acc_ref, idx=Ellipsis):
    return acc_ref[idx] / l_ref[idx]


def _pipelined_tiles(lo, hi, produce, consume, consume_last=None):
    consume_last = consume_last or consume
    produce(lo, 0)

    def pair(j, carry):
        kt = lo + 2 * j
        produce(kt + 1, 1)
        consume(kt, 0)
        produce(kt + 2, 0)
        consume(kt + 1, 1)
        return carry

    n_body = hi - lo - 1
    lax.fori_loop(0, n_body // 2, pair, 0)

    @pl.when(n_body % 2 == 0)
    def _():
        consume_last(hi - 1, 0)

    @pl.when(n_body % 2 == 1)
    def _():
        produce(hi - 1, 1)
        consume(hi - 2, 0)
        consume_last(hi - 1, 1)


def _stack_heads(q, n_heads):
    return jnp.concatenate([q[:, h * HEAD_DIM:(h + 1) * HEAD_DIM] for h in range(n_heads)], axis=0)


def _store_heads(o_t, o_ref, n_heads, tq):
    for j in range(n_heads // 2):
        pair = jnp.concatenate([o_t[:, (2 * j) * tq:(2 * j + 1) * tq],
                                o_t[:, (2 * j + 1) * tq:(2 * j + 2) * tq]], axis=0)
        o_ref[0, :, 2 * j * HEAD_DIM:(2 * j + 2) * HEAD_DIM] = pair.T.astype(o_ref.dtype)


def _dsa_kernel(qi_ref, w_ref, ki_ref, qa_ref, ka_ref, vat_ref, o_ref,
                key_ref, m_ref, l_ref, acc_ref, sbuf_ref, *, k_sel, idx_scale):
    i = pl.program_id(1)
    tq = QBLOCK
    tk = KEY_TILE
    n_tiles = (i * tq + tq + tk - 1) // tk
    q0 = i * tq
    t_row = q0 + lax.broadcasted_iota(jnp.int32, (1, tq), 1)
    kk = lax.broadcasted_iota(jnp.int32, (tk, tq), 0)

    qi = _stack_heads(qi_ref[0], IDX_HEADS)
    w_row = w_ref[0, 0]

    def tile_start(kt):
        return pl.multiple_of(kt * tk, tk)

    def logits_tile(kt, slot):
        sbuf_ref[slot] = _dot_nt(ki_ref[0, pl.ds(tile_start(kt), tk), :], qi)

    def score_tile(kt, slot):
        k0 = tile_start(kt)
        z = jnp.maximum(sbuf_ref[slot], 0.0) * w_row
        sc = z[:, 0:tq]
        for h in range(1, IDX_HEADS):
            sc = sc + z[:, h * tq:(h + 1) * tq]
        sc = sc * idx_scale
        sc = jnp.where(k0 + kk <= t_row, sc, NEG)
        bits = lax.bitcast_convert_type(sc, jnp.int32)
        key_ref[pl.ds(k0, tk), :] = _order_key(bits)

    _pipelined_tiles(0, n_tiles, logits_tile, score_tile)

    @pl.when(n_tiles % 2 == 1)
    def _():
        key_ref[pl.ds(tile_start(n_tiles), tk), :] = jnp.full((tk, tq), NEG_KEY, jnp.int32)

    ct = 2 * tk
    kk2 = lax.broadcasted_iota(jnp.int32, (ct, tq), 0)

    def count(pred_fn):
        def body(kt, acc):
            k0 = pl.multiple_of(kt * ct, ct)
            c = pred_fn(key_ref[pl.ds(k0, ct), :], k0 + kk2).astype(jnp.int32)
            return acc + jnp.sum(c.reshape(ct // SUBLANES, SUBLANES, tq), axis=0)
        acc = lax.fori_loop(0, (n_tiles + 1) // 2, body, jnp.zeros((SUBLANES, tq), jnp.int32))
        return jnp.sum(acc, axis=0, keepdims=True)

    def search(_):
        c0 = count(lambda key, idx: key >= 0)
        thr = jnp.where(c0 >= k_sel, 0, INT_MIN).astype(jnp.int32)

        def bit_step(b, thr):
            trial = thr | (jnp.int32(1) << (30 - b))
            c = count(lambda key, idx: key >= trial)
            return jnp.where(c >= k_sel, trial, thr)

        thr = lax.fori_loop(0, 31, bit_step, thr)
        c_ge = count(lambda key, idx: key >= thr)

        def tie_search(_):
            c_gt = count(lambda key, idx: key > thr)
            need = k_sel - c_gt

            def idx_step(b, u):
                trial = u | (jnp.int32(1) << (20 - b))
                c = count(lambda key, idx: (key == thr) & (idx < trial))
                return jnp.where(c < need, trial, u)

            return lax.fori_loop(0, 21, idx_step, jnp.zeros((1, tq), jnp.int32))

        cut = lax.cond(jnp.max(c_ge) > k_sel, tie_search,
                       lambda _: jnp.full((1, tq), IDX_ALL, jnp.int32), 0)
        return thr, cut

    thr, cut = lax.cond(q0 + tq > k_sel, search,
                        lambda _: (jnp.full((1, tq), INT_MIN, jnp.int32),
                                   jnp.full((1, tq), IDX_ALL, jnp.int32)), 0)

    qa = _stack_heads(qa_ref[0], A_HEADS)
    _init_state(m_ref, l_ref, acc_ref)

    def qk_tile(kt, slot):
        sbuf_ref[slot] = _dot_nt(ka_ref[0, pl.ds(tile_start(kt), tk), :], qa)

    def attn_tile(kt, slot):
        k0 = tile_start(kt)
        key = key_ref[pl.ds(k0, tk), :]
        idx = k0 + kk
        sel = ((key > thr) | ((key == thr) & (idx <= cut))) & (idx <= t_row)
        _online_update(sbuf_ref[slot], _drop(sel, A_HEADS), vat_ref[0, kt], m_ref, l_ref, acc_ref)

    _pipelined_tiles(0, n_tiles, qk_tile, attn_tile)
    _store_heads(_normalized(l_ref, acc_ref), o_ref, A_HEADS, tq)


def _value_tiles_t(v, tk):
    b, s, d = v.shape
    return jnp.transpose(v.reshape(b, s // tk, tk, d), (0, 1, 3, 2)).astype(MXU_DTYPE)


def _per_query_rows(x, tq):
    b, s, h = x.shape
    return jnp.transpose(x.reshape(b, s // tq, tq, h), (0, 1, 3, 2)).reshape(b, s // tq, 1, h * tq)


def _dsa_attention(q_a, k_a, v_a, q_i, k_i, w_i):
    b, s, _ = q_a.shape
    tq = QBLOCK
    nq = s // tq
    k_sel = min(DSA_TOPK_MAX, s // 4)
    tk = KEY_TILE
    assert (s // tk) % 2 == 0
    kern = functools.partial(_dsa_kernel, k_sel=k_sel, idx_scale=float((IDX_HEADS * IDX_DIM) ** -0.5))
    hq = A_HEADS * tq
    return pl.pallas_call(
        kern,
        out_shape=jax.ShapeDtypeStruct((b, s, A_HEADS * HEAD_DIM), MXU_DTYPE),
        grid=(b, nq),
        in_specs=[
            pl.BlockSpec((1, tq, IDX_HEADS * IDX_DIM), lambda bi, i: (bi, i, 0)),
            pl.BlockSpec((1, 1, 1, hq), lambda bi, i: (bi, i, 0, 0)),
            pl.BlockSpec((1, s, IDX_DIM), lambda bi, i: (bi, 0, 0)),
            pl.BlockSpec((1, tq, A_HEADS * HEAD_DIM), lambda bi, i: (bi, i, 0)),
            pl.BlockSpec((1, s, HEAD_DIM), lambda bi, i: (bi, 0, 0)),
            pl.BlockSpec((1, s // tk, HEAD_DIM, tk), lambda bi, i: (bi, 0, 0, 0)),
        ],
        out_specs=pl.BlockSpec((1, tq, A_HEADS * HEAD_DIM), lambda bi, i: (bi, i, 0)),
        scratch_shapes=[pltpu.VMEM((s, tq), jnp.int32),
                        pltpu.VMEM((1, hq), jnp.float32),
                        pltpu.VMEM((1, hq), jnp.float32),
                        pltpu.VMEM((HEAD_DIM, hq), jnp.float32),
                        pltpu.VMEM((2, tk, hq), jnp.float32)],
        compiler_params=_cparams("parallel", "arbitrary"),
        name="dsa_attention",
    )(q_i, _per_query_rows(w_i.astype(jnp.float32), tq), k_i.astype(MXU_DTYPE), q_a, k_a.astype(MXU_DTYPE),
      _value_tiles_t(v_a, tk))


def _cmp_kernel(ck_ref, pe_ref, w1_ref, w2_ref, o_ref):
    nc = ck_ref.shape[1]
    half = ck_ref.shape[2]
    ck = ck_ref[0]
    a = _dot((ck + pe_ref[0:1, :]).astype(MXU_DTYPE), w1_ref[0:half, :])
    bm = _dot((ck + pe_ref[1:2, :]).astype(MXU_DTYPE), w1_ref[half:2 * half, :])
    h = a + pltpu.roll(bm, nc - 1, 0)
    o_ref[0] = _dot(_silu(h).astype(MXU_DTYPE), w2_ref[...])


def _compress(kv, pe, w1, w2):
    b, s, d = kv.shape
    nc = s // CMP_STRIDE
    half = CMP_STRIDE * d
    ck = kv.reshape(b, nc, half)
    pe2 = pe.reshape(2, half)
    return pl.pallas_call(
        _cmp_kernel,
        out_shape=jax.ShapeDtypeStruct((b, nc, d), jnp.float32),
        grid=(b,),
        in_specs=[pl.BlockSpec((1, nc, half), lambda bi: (bi, 0, 0)),
                  pl.BlockSpec((2, half), lambda bi: (0, 0)),
                  pl.BlockSpec((2 * half, d), lambda bi: (0, 0)),
                  pl.BlockSpec((d, d), lambda bi: (0, 0))],
        out_specs=pl.BlockSpec((1, nc, d), lambda bi: (bi, 0, 0)),
        compiler_params=_cparams("parallel"),
        name="nsa_compress",
    )(ck, pe2, w1.astype(MXU_DTYPE), w2.astype(MXU_DTYPE))


def _split3(x):
    def top(v):
        return lax.bitcast_convert_type(lax.bitcast_convert_type(v, jnp.int32) & jnp.int32(-65536), jnp.float32)
    hi = top(x)
    r1 = x - hi
    mid = top(r1)
    lo = r1 - mid
    return hi.astype(MXU_DTYPE), mid.astype(MXU_DTYPE), lo.astype(MXU_DTYPE)


def _nsa_kernel(q_ref, g_ref, kc_ref, vct_ref, ovt_ref, exp_ref, ks_ref, vst_ref, kw_ref, vwt_ref, o_ref,
                tok_ref, m_ref, l_ref, acc_ref, m2_ref, l2_ref, acc2_ref, out_ref, sbuf_ref, *, n_sel, n_cmp):
    i = pl.program_id(1)
    tq = QBLOCK
    tk = KEY_TILE
    nh = B_HEADS
    hq = nh * tq
    q0 = i * tq
    n_tiles = (q0 + tq + tk - 1) // tk
    t_row = q0 + lax.broadcasted_iota(jnp.int32, (1, tq), 1)
    kk = lax.broadcasted_iota(jnp.int32, (tk, tq), 0)
    q = _stack_heads(q_ref[0], nh)
    gates = g_ref[0, 0]

    nc = kc_ref.shape[1]
    c_id = lax.broadcasted_iota(jnp.int32, (nc, tq), 0)
    valid_c = (c_id * CMP_STRIDE + (CMP_LEN - 1) <= t_row) & (c_id < n_cmp)
    keep_c = jnp.concatenate([jnp.where(valid_c, 1.0, 0.0)] * nh, axis=1)
    s_c = _dot_nt(kc_ref[0], q) + _drop(valid_c, nh)
    e_c = jnp.exp2(s_c - jnp.max(s_c, axis=0, keepdims=True)) * keep_c
    den = jnp.sum(e_c, axis=0, keepdims=True)
    p_c = e_c / jnp.where(den > 0.0, den, 1.0)
    out_ref[...] = gates[0:1, :] * _dot(vct_ref[0], p_c.astype(MXU_DTYPE))

    p_sum = p_c[:, 0:tq]
    for h in range(1, nh):
        p_sum = p_sum + p_c[:, h * tq:(h + 1) * tq]
    ovt = ovt_ref[...]
    pieces = _split3(p_sum)
    imp = _dot(ovt, pieces[0]) + _dot(ovt, pieces[1]) + _dot(ovt, pieces[2])
    n_sb = ovt.shape[0]
    j_id = lax.broadcasted_iota(jnp.int32, (n_sb, tq), 0)
    cur = t_row >> (SEL_LEN.bit_length() - 1)
    forced = (j_id == 0) | (j_id == cur) | (j_id == cur - 1)
    future = j_id * SEL_LEN > t_row
    imp = jnp.where(forced, BIG, jnp.where(future, NEG, imp))
    chosen = jnp.zeros((n_sb, tq), jnp.bool_)
    for _ in range(n_sel):
        cand = jnp.where(chosen, -jnp.inf, imp)
        best = jnp.max(cand, axis=0, keepdims=True)
        first = jnp.min(jnp.where(cand == best, j_id, n_sb), axis=0, keepdims=True)
        chosen = chosen | (j_id == first)
    tok_ref[...] = _dot(exp_ref[...], jnp.where(chosen, 1.0, 0.0).astype(MXU_DTYPE))

    _init_state(m_ref, l_ref, acc_ref)
    _init_state(m2_ref, l2_ref, acc2_ref)
    win_lo = jnp.maximum(q0 - WINDOW, 0) // tk

    def tile_start(kt):
        return pl.multiple_of(kt * tk, tk)

    def sel_qk(kt, slot):
        sbuf_ref[slot] = _dot_nt(ks_ref[0, pl.ds(tile_start(kt), tk), :], q)

    def sel_tile(kt, slot):
        k0 = tile_start(kt)
        sel = (tok_ref[pl.ds(k0, tk), :] > 0.5) & (k0 + kk <= t_row)
        _online_update(sbuf_ref[slot], _drop(sel, nh), vst_ref[0, kt], m_ref, l_ref, acc_ref)

    def win_qk(kt, slot):
        sbuf_ref[slot] = _dot_nt(kw_ref[0, pl.ds(tile_start(kt), tk), :], q)

    def win_tile(kt, slot):
        kpos = tile_start(kt) + kk
        ok = (kpos <= t_row) & (kpos > t_row - WINDOW)
        _online_update(sbuf_ref[slot], _drop(ok, nh), vwt_ref[0, kt], m2_ref, l2_ref, acc2_ref)

    _pipelined_tiles(0, n_tiles, sel_qk, sel_tile)
    _pipelined_tiles(win_lo, n_tiles, win_qk, win_tile)
    o_t = out_ref[...] + gates[1:2, :] * _normalized(l_ref, acc_ref) + gates[2:3, :] * _normalized(l2_ref, acc2_ref)
    _store_heads(o_t, o_ref, nh, tq)


def _nsa_attention(q_b, k_c, v_c, k_s, v_s, k_w, v_w, gate):
    b, s, _ = q_b.shape
    tq = QBLOCK
    nq = s // tq
    nh = B_HEADS
    hq = nh * tq
    nc = k_c.shape[1]
    n_cmp = (s - CMP_LEN) // CMP_STRIDE + 1
    n_sb = s // SEL_LEN
    n_sel = min(SEL_BLOCKS_MAX, n_sb)
    c_start = np.arange(nc) * CMP_STRIDE
    s_start = np.arange(n_sb) * SEL_LEN
    ov = np.clip(np.minimum(c_start[:, None] + CMP_LEN, s_start[None, :] + SEL_LEN)
                 - np.maximum(c_start[:, None], s_start[None, :]), 0, None) / CMP_LEN
    ov[n_cmp:] = 0.0
    ovt = jnp.asarray(ov.T, dtype=MXU_DTYPE)
    expand = jnp.asarray((np.arange(s)[:, None] // SEL_LEN == np.arange(n_sb)[None, :]), dtype=MXU_DTYPE)
    g = _sigmoid(gate.astype(jnp.float32))
    g = jnp.transpose(g.reshape(b, nq, tq, nh, 3), (0, 1, 4, 3, 2)).reshape(b, nq, 3, hq)
    tk = KEY_TILE
    kern = functools.partial(_nsa_kernel, n_sel=n_sel, n_cmp=n_cmp)
    full = lambda shape: pl.BlockSpec(shape, lambda bi, i: (0,) * len(shape))
    per_b = lambda shape: pl.BlockSpec(shape, lambda bi, i: (bi,) + (0,) * (len(shape) - 1))
    state = [pltpu.VMEM((1, hq), jnp.float32), pltpu.VMEM((1, hq), jnp.float32),
             pltpu.VMEM((HEAD_DIM, hq), jnp.float32)]
    return pl.pallas_call(
        kern,
        out_shape=jax.ShapeDtypeStruct((b, s, nh * HEAD_DIM), MXU_DTYPE),
        grid=(b, nq),
        in_specs=[
            pl.BlockSpec((1, tq, nh * HEAD_DIM), lambda bi, i: (bi, i, 0)),
            pl.BlockSpec((1, 1, 3, hq), lambda bi, i: (bi, i, 0, 0)),
            per_b((1, nc, HEAD_DIM)),
            per_b((1, HEAD_DIM, nc)),
            full((n_sb, nc)),
            full((s, n_sb)),
            per_b((1, s, HEAD_DIM)),
            per_b((1, s // tk, HEAD_DIM, tk)),
            per_b((1, s, HEAD_DIM)),
            per_b((1, s // tk, HEAD_DIM, tk)),
        ],
        out_specs=pl.BlockSpec((1, tq, nh * HEAD_DIM), lambda bi, i: (bi, i, 0)),
        scratch_shapes=[pltpu.VMEM((s, tq), jnp.float32)] + state + state
                       + [pltpu.VMEM((HEAD_DIM, hq), jnp.float32), pltpu.VMEM((2, tk, hq), jnp.float32)],
        compiler_params=_cparams("parallel", "arbitrary"),
        name="nsa_attention",
    )(q_b, g, k_c.astype(MXU_DTYPE),
      jnp.transpose(v_c, (0, 2, 1)).astype(MXU_DTYPE), ovt, expand,
      k_s.astype(MXU_DTYPE), _value_tiles_t(v_s, tk), k_w.astype(MXU_DTYPE), _value_tiles_t(v_w, tk))


FOX_TQ = 256
FOX_HB = 4
FOX_KPAD = 128


def _fox_kernel(q_ref, k_ref, vt_ref, o_ref, m_ref, l_ref, acc_ref, sbuf_ref):
    i = pl.program_id(2)
    tq = FOX_TQ
    _init_state(m_ref, l_ref, acc_ref)

    def qk_tile(kt, slot):
        k0 = pl.multiple_of(kt * tq, tq)
        for hh in range(FOX_HB):
            sbuf_ref[slot, hh] = _dot_nt(k_ref[0, hh, pl.ds(k0, tq), :], q_ref[0, hh])

    def update(kt, slot, drop):
        for hh in range(FOX_HB):
            _online_update(sbuf_ref[slot, hh], drop, vt_ref[0, hh, kt], m_ref, l_ref, acc_ref, idx=hh)

    def diag_tile(kt, slot):
        kk = lax.broadcasted_iota(jnp.int32, (tq, tq), 0)
        qq = lax.broadcasted_iota(jnp.int32, (tq, tq), 1)
        update(kt, slot, _drop(kk <= qq))

    _pipelined_tiles(0, i + 1, qk_tile, lambda kt, slot: update(kt, slot, None), diag_tile)
    for j in range(FOX_HB // 2):
        pair = jnp.concatenate([_normalized(l_ref, acc_ref, 2 * j), _normalized(l_ref, acc_ref, 2 * j + 1)], axis=0)
        o_ref[0, :, 2 * j * HEAD_DIM:(2 * j + 2) * HEAD_DIM] = pair.T.astype(o_ref.dtype)


def _fox_attention(q, k, v, log_f):
    b, s, _ = q.shape
    nh = C_HEADS
    tq = min(FOX_TQ, s)
    assert tq == FOX_TQ
    d_cum = jnp.cumsum(log_f, axis=1)
    d3 = jnp.stack(_split3(d_cum * LOG2E), axis=-1)
    ones = jnp.ones((b, s, nh, 3), MXU_DTYPE)
    pad = jnp.zeros((b, s, nh, FOX_KPAD - HEAD_DIM - 3), MXU_DTYPE)
    qh = (q * ATTN_SCALE).reshape(b, s, nh, HEAD_DIM).astype(MXU_DTYPE)
    kh = k.reshape(b, s, nh, HEAD_DIM).astype(MXU_DTYPE)
    q_aug = jnp.transpose(jnp.concatenate([qh, ones, pad], axis=-1), (0, 2, 1, 3))
    k_aug = jnp.transpose(jnp.concatenate([kh, -d3, pad], axis=-1), (0, 2, 1, 3))
    vt = jnp.transpose(v.reshape(b, s // tq, tq, nh, HEAD_DIM), (0, 3, 1, 4, 2)).astype(MXU_DTYPE)
    return pl.pallas_call(
        _fox_kernel,
        out_shape=jax.ShapeDtypeStruct((b, s, nh * HEAD_DIM), MXU_DTYPE),
        grid=(b, nh // FOX_HB, s // tq),
        in_specs=[
            pl.BlockSpec((1, FOX_HB, tq, FOX_KPAD), lambda bi, j, i: (bi, j, i, 0)),
            pl.BlockSpec((1, FOX_HB, s, FOX_KPAD), lambda bi, j, i: (bi, j, 0, 0)),
            pl.BlockSpec((1, FOX_HB, s // tq, HEAD_DIM, tq), lambda bi, j, i: (bi, j, 0, 0, 0)),
        ],
        out_specs=pl.BlockSpec((1, tq, FOX_HB * HEAD_DIM), lambda bi, j, i: (bi, i, j)),
        scratch_shapes=[pltpu.VMEM((FOX_HB, 1, tq), jnp.float32),
                        pltpu.VMEM((FOX_HB, 1, tq), jnp.float32),
                        pltpu.VMEM((FOX_HB, HEAD_DIM, tq), jnp.float32),
                        pltpu.VMEM((2, FOX_HB, tq, tq), jnp.float32)],
        compiler_params=_cparams("parallel", "parallel", "arbitrary"),
        name="fox_attention",
    )(q_aug, k_aug, vt)


def _layer_norm(y, g, b):
    mu = jnp.mean(y, axis=-1, keepdims=True)
    d = y - mu
    var = jnp.mean(d * d, axis=-1, keepdims=True)
    return d * lax.rsqrt(var + LN_EPS) * g + b


def _mid_kernel(o_ref, x_ref, wo_ref, g_ref, b_ref, sw1_ref, sw3_ref, sw2_ref, rw_ref,
                x1_ref, x1b_ref, sh_ref, lg_ref, *, alpha):
    y = alpha * x_ref[...] + _dot(o_ref[...], wo_ref[...])
    x1 = _layer_norm(y, g_ref[...], b_ref[...])
    x1_ref[...] = x1
    xb = x1.astype(MXU_DTYPE)
    x1b_ref[...] = xb
    hid = _silu(_dot(xb, sw1_ref[...])) * _dot(xb, sw3_ref[...])
    sh_ref[...] = _dot(hid.astype(MXU_DTYPE), sw2_ref[...])
    lg_ref[...] = _dot_nt(rw_ref[...], xb)


def _mid_block(o, x, w_o, ln_g, ln_b, sw1, sw3, sw2, router_w, alpha):
    t, d = x.shape
    f = sw1.shape[1]
    e = router_w.shape[1]
    tm = _pick_tm(t, 512)
    row = lambda n: pl.BlockSpec((tm, n), lambda i: (i, 0))
    res = lambda a, c: pl.BlockSpec((a, c), lambda i: (0, 0))
    return pl.pallas_call(
        functools.partial(_mid_kernel, alpha=alpha),
        out_shape=(jax.ShapeDtypeStruct((t, d), jnp.float32),
                   jax.ShapeDtypeStruct((t, d), MXU_DTYPE),
                   jax.ShapeDtypeStruct((t, d), jnp.float32),
                   jax.ShapeDtypeStruct((e, t), jnp.float32)),
        grid=(t // tm,),
        in_specs=[row(o.shape[1]), row(d), res(o.shape[1], d), res(1, d), res(1, d),
                  res(d, f), res(d, f), res(f, d), res(e, d)],
        out_specs=(row(d), row(d), row(d), pl.BlockSpec((e, tm), lambda i: (0, i))),
        compiler_params=_cparams("parallel"),
        name="mixer_out_ln_shared_router",
    )(o, x, w_o.astype(MXU_DTYPE), ln_g.reshape(1, d), ln_b.reshape(1, d),
      sw1.astype(MXU_DTYPE), sw3.astype(MXU_DTYPE), sw2.astype(MXU_DTYPE), router_w.T.astype(MXU_DTYPE))


def _expert_kernel(blk_e_ref, xs_ref, w1_ref, w3_ref, w2_ref, y_ref):
    xe = xs_ref[...]
    hid = _silu(_dot(xe, w1_ref[0])) * _dot(xe, w3_ref[0])
    y_ref[...] = _dot(hid.astype(MXU_DTYPE), w2_ref[0]).astype(y_ref.dtype)


def _grouped_experts(xs, blk_e, w1, w3, w2):
    n_slots, d = xs.shape
    f = w1.shape[2]
    n_blocks = n_slots // EXPERT_BLOCK
    return pl.pallas_call(
        _expert_kernel,
        out_shape=jax.ShapeDtypeStruct((n_slots, d), MXU_DTYPE),
        grid_spec=pltpu.PrefetchScalarGridSpec(
            num_scalar_prefetch=1,
            grid=(n_blocks,),
            in_specs=[pl.BlockSpec((EXPERT_BLOCK, d), lambda i, be: (i, 0)),
                      pl.BlockSpec((1, d, f), lambda i, be: (be[i], 0, 0)),
                      pl.BlockSpec((1, d, f), lambda i, be: (be[i], 0, 0)),
                      pl.BlockSpec((1, f, d), lambda i, be: (be[i], 0, 0))],
            out_specs=pl.BlockSpec((EXPERT_BLOCK, d), lambda i, be: (i, 0)),
        ),
        compiler_params=_cparams("arbitrary"),
        name="routed_experts",
    )(blk_e, xs, w1.astype(MXU_DTYPE), w3.astype(MXU_DTYPE), w2.astype(MXU_DTYPE))


ROUTER_TM = 512


def _pick_rows(rows, row_id, n):
    out = jnp.zeros((n,) + rows[0].shape[1:], rows[0].dtype)
    for r in range(n):
        out = jnp.where(row_id == r, rows[r], out)
    return out


def _router_kernel(lg_ref, bias_ref, tri_ref, eidx_ref, gate_ref, rank_ref, cnt_ref, carry_ref):
    e, tm = lg_ref.shape
    per_group = e // N_GROUPS

    @pl.when(pl.program_id(0) == 0)
    def _():
        carry_ref[...] = jnp.zeros(carry_ref.shape, jnp.float32)

    s = _sigmoid(lg_ref[...])
    sb = s + bias_ref[...]
    neg_inf = -jnp.inf

    sub_id = lax.broadcasted_iota(jnp.int32, (per_group, tm), 0)
    g_rows = []
    for g in range(N_GROUPS):
        blk = sb[g * per_group:(g + 1) * per_group, :]
        m1 = jnp.max(blk, axis=0, keepdims=True)
        f1 = jnp.min(jnp.where(blk == m1, sub_id, per_group), axis=0, keepdims=True)
        m2 = jnp.max(jnp.where(sub_id == f1, neg_inf, blk), axis=0, keepdims=True)
        g_rows.append(m1 + m2)
    g_id = lax.broadcasted_iota(jnp.int32, (N_GROUPS, tm), 0)
    gscore = _pick_rows(g_rows, g_id, N_GROUPS)

    e_id = lax.broadcasted_iota(jnp.int32, (e, tm), 0)
    e_group = e_id // per_group if per_group & (per_group - 1) else e_id >> (per_group.bit_length() - 1)
    g_taken = jnp.zeros((N_GROUPS, tm), jnp.bool_)
    e_allowed = jnp.zeros((e, tm), jnp.bool_)
    for _ in range(TOPK_GROUPS):
        cand = jnp.where(g_taken, neg_inf, gscore)
        best = jnp.max(cand, axis=0, keepdims=True)
        first = jnp.min(jnp.where(cand == best, g_id, N_GROUPS), axis=0, keepdims=True)
        g_taken = g_taken | (g_id == first)
        e_allowed = e_allowed | (e_group == first)

    masked = jnp.where(e_allowed, sb, NEG)
    chosen = jnp.zeros((e, tm), jnp.bool_)
    id_rows, sel_rows = [], []
    for _ in range(TOP_K):
        cand = jnp.where(chosen, neg_inf, masked)
        best = jnp.max(cand, axis=0, keepdims=True)
        first = jnp.min(jnp.where(cand == best, e_id, e), axis=0, keepdims=True)
        hit = e_id == first
        chosen = chosen | hit
        id_rows.append(first)
        sel_rows.append(jnp.sum(jnp.where(hit, s, 0.0), axis=0, keepdims=True))
    k_id = lax.broadcasted_iota(jnp.int32, (TOP_K, tm), 0)
    eidx = _pick_rows(id_rows, k_id, TOP_K)
    sel = _pick_rows(sel_rows, k_id, TOP_K)
    eidx_ref[...] = eidx
    gate_ref[...] = sel / jnp.sum(sel, axis=0, keepdims=True) * ROUTED_SCALE

    chosen_f = jnp.where(chosen, 1.0, 0.0)
    incl = _dot(chosen_f.astype(MXU_DTYPE), tri_ref[...])
    rank_dense = carry_ref[...] + incl - chosen_f
    rank_rows = [jnp.sum(jnp.where(e_id == id_rows[r], rank_dense, 0.0), axis=0, keepdims=True)
                 for r in range(TOP_K)]
    rank_ref[...] = _pick_rows(rank_rows, k_id, TOP_K).astype(jnp.int32)
    carry_ref[...] = carry_ref[...] + jnp.sum(chosen_f, axis=1, keepdims=True)
    cnt_ref[...] = carry_ref[...]


def _route(logits_t, router_bias):
    e, t = logits_t.shape
    tm = _pick_tm(t, ROUTER_TM)
    tri = jnp.asarray(np.triu(np.ones((tm, tm), np.float32)), dtype=MXU_DTYPE)
    kt = lambda dt: jax.ShapeDtypeStruct((TOP_K, t), dt)
    col = pl.BlockSpec((TOP_K, tm), lambda i: (0, i))
    eidx, gates, rank, cnt = pl.pallas_call(
        _router_kernel,
        out_shape=(kt(jnp.int32), kt(jnp.float32), kt(jnp.int32), jax.ShapeDtypeStruct((e, 1), jnp.float32)),
        grid=(t // tm,),
        in_specs=[pl.BlockSpec((e, tm), lambda i: (0, i)),
                  pl.BlockSpec((e, 1), lambda i: (0, 0)),
                  pl.BlockSpec((tm, tm), lambda i: (0, 0))],
        out_specs=(col, col, col, pl.BlockSpec((e, 1), lambda i: (0, 0))),
        scratch_shapes=[pltpu.VMEM((e, 1), jnp.float32)],
        compiler_params=_cparams("arbitrary"),
        name="moe_router",
    )(logits_t, router_bias.astype(jnp.float32).reshape(e, 1), tri)
    return eidx, gates, rank, cnt[:, 0].astype(jnp.int32)


def _moe_routed(x1b, eidx, rank, counts, w1, w3, w2):
    n_tok, d = x1b.shape
    tk = n_tok * TOP_K
    n_blocks = (tk + N_EXPERTS * (EXPERT_BLOCK - 1)) // EXPERT_BLOCK + 1
    n_slots = n_blocks * EXPERT_BLOCK
    padded = (counts + EXPERT_BLOCK - 1) // EXPERT_BLOCK * EXPERT_BLOCK
    e_ids = jnp.arange(N_EXPERTS, dtype=jnp.int32)
    pad_end = jnp.sum(jnp.where(e_ids[None, :] <= e_ids[:, None], padded[None, :], 0), axis=1)
    start_pad = (pad_end - padded).astype(jnp.int32)
    start_of = jnp.sum(jnp.where(eidx[None] == e_ids[:, None, None], start_pad[:, None, None], 0), axis=0)
    dest = start_of + rank
    tok = jnp.broadcast_to(jnp.arange(n_tok, dtype=jnp.int32)[None, :], (TOP_K, n_tok))
    slot_tok = jnp.zeros((n_slots,), jnp.int32).at[dest.reshape(tk)].set(
        tok.reshape(tk), unique_indices=True, mode="drop")
    blk_start = jnp.arange(n_blocks, dtype=jnp.int32) * EXPERT_BLOCK
    blk_e = jnp.minimum(jnp.sum((pad_end[None, :] <= blk_start[:, None]).astype(jnp.int32), axis=1),
                        N_EXPERTS - 1).astype(jnp.int32)
    y = _grouped_experts(x1b[slot_tok], blk_e, w1, w3, w2)
    return y, dest


def _post_kernel(x1_ref, sh_ref, yg_ref, gt_ref, p_ref, g_ref, b_ref, wp_ref, ng_ref, wg_ref, o_ref, *, alpha):
    gates = gt_ref[...]
    routed = yg_ref[0].astype(jnp.float32) * gates[:, 0:1]
    for k in range(1, TOP_K):
        routed = routed + yg_ref[k].astype(jnp.float32) * gates[:, k:k + 1]
    y = alpha * x1_ref[...] + (sh_ref[...] + routed)
    x2 = _layer_norm(y, g_ref[...], b_ref[...])
    e = _dot(p_ref[...].astype(MXU_DTYPE), wp_ref[...])
    e = e * lax.rsqrt(jnp.mean(e * e, axis=-1, keepdims=True) + RMS_EPS) * ng_ref[...]
    gate = _sigmoid(_dot(x2.astype(MXU_DTYPE), wg_ref[...]))
    o_ref[...] = x2 + gate * e


def _post_block(x1, shared, y_pairs, gates_t, p, ln_g, ln_b, w_proj, norm_g, w_gate, alpha):
    t, d = x1.shape
    pd = p.shape[1]
    tm = _pick_tm(t, 256)
    row = lambda n: pl.BlockSpec((tm, n), lambda i: (i, 0))
    res = lambda a, c: pl.BlockSpec((a, c), lambda i: (0, 0))
    return pl.pallas_call(
        functools.partial(_post_kernel, alpha=alpha),
        out_shape=jax.ShapeDtypeStruct((t, d), jnp.float32),
        grid=(t // tm,),
        in_specs=[row(d), row(d), pl.BlockSpec((TOP_K, tm, d), lambda i: (0, i, 0)), row(TOP_K), row(pd),
                  res(1, d), res(1, d), res(pd, d), res(1, d), res(d, d)],
        out_specs=row(d),
        compiler_params=_cparams("parallel"),
        name="moe_combine_ln_ple",
    )(x1, shared, y_pairs, gates_t, p, ln_g.reshape(1, d), ln_b.reshape(1, d), w_proj.astype(MXU_DTYPE),
      norm_g.reshape(1, d), w_gate.astype(MXU_DTYPE))


def _rope_tables(s_len):
    inv = 1.0 / (ROPE_THETA ** (np.arange(0, HEAD_DIM, 2, dtype=np.float32) / HEAD_DIM))
    ang = jnp.arange(s_len, dtype=jnp.float32)[:, None] * jnp.asarray(inv, dtype=jnp.float32)[None, :]
    return jnp.cos(ang), jnp.sin(ang)


def _rope(x, cos, sin):
    b, l, w = x.shape
    xh = x.reshape(b, l, w // HEAD_DIM, HEAD_DIM)
    half = HEAD_DIM // 2
    x1, x2 = xh[..., :half], xh[..., half:]
    c = cos[:, None, :]
    s = sin[:, None, :]
    return jnp.concatenate([x1 * c - x2 * s, x2 * c + x1 * s], axis=-1).reshape(b, l, w)


def _split_cols(h, sizes):
    out, c = [], 0
    for n in sizes:
        out.append(h[..., c:c + n])
        c += n
    return out


def _even_mixer(x, w_in, b_gate, pos_k, w1_k, w2_k, pos_v, w1_v, w2_v, cos, sin):
    b, s, d = x.shape
    mix = A_HEADS * HEAD_DIM
    q_a, q_i, q_b, rest = _even_projection(x.reshape(b * s, d), w_in, s)
    q_a, q_i, q_b = (q.reshape(b, s, mix) for q in (q_a, q_i, q_b))
    k_a, v_a, k_i, w_i, kv_b, g_b = _split_cols(rest.reshape(b, s, -1),
                                                (HEAD_DIM, HEAD_DIM, IDX_DIM, IDX_HEADS, 6 * HEAD_DIM, 3 * B_HEADS))
    o_a = _dsa_attention(q_a, _rope(k_a, cos, sin), v_a, q_i, _rope(k_i, cos, sin), w_i)
    k_c, v_c, k_s, v_s, k_w, v_w = _split_cols(kv_b, (HEAD_DIM,) * 6)
    nc = s // CMP_STRIDE
    c_last = jnp.minimum(jnp.arange(nc) * CMP_STRIDE + CMP_LEN - 1, s - 1)
    kc = _rope(_compress(k_c, pos_k, w1_k, w2_k), cos[c_last], sin[c_last])
    vc = _compress(v_c, pos_v, w1_v, w2_v)
    gate = (g_b + b_gate).reshape(b, s, B_HEADS, 3)
    o_b = _nsa_attention(q_b, kc, vc, _rope(k_s, cos, sin), v_s, _rope(k_w, cos, sin), v_w, gate)
    return jnp.concatenate([o_a, o_b], axis=-1).reshape(b * s, 2 * mix)


def _odd_mixer(x, w_in, b_f):
    b, s, d = x.shape
    mix = C_HEADS * HEAD_DIM
    h = _matmul(x.reshape(b * s, d), w_in, name="odd_in_proj").reshape(b, s, -1)
    q, k, v, f = _split_cols(h, (mix, mix, mix, C_HEADS))
    log_f = jax.nn.log_sigmoid((f + b_f).astype(jnp.float32))
    return _fox_attention(q, k, v, log_f).reshape(b * s, mix)


def kernel(x, p, ev_w_in, ev_b_gate, ev_cmp_pos_k, ev_cmp_w1_k, ev_cmp_w2_k, ev_cmp_pos_v, ev_cmp_w1_v, ev_cmp_w2_v, ev_w_o, od_w_in, od_b_f, od_w_o, ln1_g, ln1_b, ln2_g, ln2_b, router_w, router_bias, exp_w1, exp_w3, exp_w2, sh_w1, sh_w3, sh_w2, ple_w_gate, ple_w_proj, ple_norm_g):
    b, s, d = x.shape
    depth = p.shape[0]
    alpha = float((2.0 * depth) ** 0.25)
    cos, sin = _rope_tables(s)
    xf = x.reshape(b * s, d)
    for i in range(depth):
        j = i // 2
        x3 = xf.reshape(b, s, d)
        if i % 2 == 0:
            o = _even_mixer(x3, ev_w_in[j], ev_b_gate[j], ev_cmp_pos_k[j], ev_cmp_w1_k[j], ev_cmp_w2_k[j],
                            ev_cmp_pos_v[j], ev_cmp_w1_v[j], ev_cmp_w2_v[j], cos, sin)
            w_o = ev_w_o[j]
        else:
            o = _odd_mixer(x3, od_w_in[j], od_b_f[j])
            w_o = od_w_o[j]
        x1, x1b, shared, logits_t = _mid_block(o, xf, w_o, ln1_g[i], ln1_b[i], sh_w1[i], sh_w3[i], sh_w2[i],
                                               router_w[i], alpha)
        eidx, gates, rank, counts = _route(logits_t, router_bias[i])
        y, dest = _moe_routed(x1b, eidx, rank, counts, exp_w1[i], exp_w3[i], exp_w2[i])
        xf = _post_block(x1, shared, y[dest], gates.T, p[i].reshape(b * s, -1), ln2_g[i], ln2_b[i],
                         ple_w_proj[i], ple_norm_g[i], ple_w_gate[i], alpha)
    return xf.reshape(b, s, d)
```

```python
import functools

import numpy as np
import jax
import jax.numpy as jnp
from jax import lax
from jax.experimental import pallas as pl
from jax.experimental.pallas import tpu as pltpu

HEAD_DIM = 64
QBLOCK = 128
ROPE_THETA = 10000.0
LN_EPS = 1e-5
RMS_EPS = 1e-6
NEG = -1e30
BIG = 1e30

A_HEADS = 8
IDX_HEADS = 8
IDX_DIM = 64
DSA_TOPK_MAX = 256

B_HEADS = 8
CMP_LEN = 32
CMP_STRIDE = 16
SEL_LEN = 32
SEL_BLOCKS_MAX = 8
WINDOW = 512

C_HEADS = 16

N_EXPERTS = 64
TOP_K = 8
N_GROUPS = 8
TOPK_GROUPS = 4
ROUTED_SCALE = 2.5
EXPERT_BLOCK = 512

LANES = 128
SUBLANES = 8
VMEM_LIMIT = 48 * 1024 * 1024

MXU_DTYPE = jnp.bfloat16
KEY_TILE = 256
ATTN_SCALE = float(HEAD_DIM ** -0.5 * np.log2(np.e))
LOG2E = float(np.log2(np.e))
INT_MIN = -(2 ** 31)
IDX_ALL = 2 ** 30


def _order_key(bits):
    return bits ^ ((bits >> 31) & 0x7FFFFFFF)


NEG_KEY = int(_order_key(np.float32(NEG).view(np.int32).astype(np.int64)).astype(np.int32))


def _cparams(*sem):
    return pltpu.CompilerParams(dimension_semantics=sem, vmem_limit_bytes=VMEM_LIMIT)


def _dot(a, b):
    return jnp.dot(a, b, preferred_element_type=jnp.float32)


def _dot_nt(a, b):
    return lax.dot_general(a, b, (((1,), (1,)), ((), ())), preferred_element_type=jnp.float32)


def _silu(x):
    return x * (1.0 / (1.0 + jnp.exp(-x)))


def _sigmoid(x):
    return 1.0 / (1.0 + jnp.exp(-x))


def _mm_kernel(x_ref, w_ref, o_ref):
    o_ref[...] = _dot(x_ref[...].astype(MXU_DTYPE), w_ref[...])


def _pick_tm(t, cap):
    tm = min(cap, t)
    while t % tm:
        tm //= 2
    return tm


def _matmul(x, w, *, tm_cap=512, name="proj"):
    t, k = x.shape
    n = w.shape[1]
    n_pad = -(-n // LANES) * LANES
    w = jnp.pad(w, ((0, 0), (0, n_pad - n))).astype(MXU_DTYPE)
    tm = _pick_tm(t, tm_cap)
    out = pl.pallas_call(
        _mm_kernel,
        out_shape=jax.ShapeDtypeStruct((t, n_pad), jnp.float32),
        grid=(t // tm,),
        in_specs=[pl.BlockSpec((tm, k), lambda i: (i, 0)),
                  pl.BlockSpec((k, n_pad), lambda i: (0, 0))],
        out_specs=pl.BlockSpec((tm, n_pad), lambda i: (i, 0)),
        compiler_params=_cparams("parallel"),
        name=name,
    )(x, w)
    return out


def _rope_lanes(x, cos, sin_signed):
    lane = lax.broadcasted_iota(jnp.int32, (x.shape[0], LANES), 1)
    first_half = (lane % HEAD_DIM) < HEAD_DIM // 2
    out = []
    for c in range(x.shape[1] // LANES):
        blk = x[:, c * LANES:(c + 1) * LANES]
        partner = jnp.where(first_half, pltpu.roll(blk, LANES - HEAD_DIM // 2, 1), pltpu.roll(blk, HEAD_DIM // 2, 1))
        out.append(blk * cos + partner * sin_signed)
    return jnp.concatenate(out, axis=1)


def _even_proj_kernel(x_ref, wq_ref, wr_ref, cos_ref, sin_ref, qa_ref, qi_ref, qb_ref, rest_ref):
    xb = x_ref[...].astype(MXU_DTYPE)
    cos = cos_ref[...]
    sin = sin_ref[...]
    for g, (o_ref, scale) in enumerate(((qa_ref, ATTN_SCALE), (qi_ref, 1.0), (qb_ref, ATTN_SCALE))):
        roped = _rope_lanes(_dot(xb, wq_ref[g]), cos, sin)
        o_ref[...] = (roped * scale).astype(o_ref.dtype)
    rest_ref[...] = _dot(xb, wr_ref[...])


def _even_projection(x, w_in, s_len):
    t, d = x.shape
    mix = A_HEADS * HEAD_DIM
    q_cols = [(0, mix), (mix + 2 * HEAD_DIM, IDX_HEADS * IDX_DIM),
              (mix + 2 * HEAD_DIM + IDX_HEADS * IDX_DIM + IDX_DIM + IDX_HEADS, B_HEADS * HEAD_DIM)]
    wq = jnp.stack([w_in[:, c:c + n] for c, n in q_cols]).astype(MXU_DTYPE)
    keep = np.ones(w_in.shape[1], bool)
    for c, n in q_cols:
        keep[c:c + n] = False
    rest_cols = np.nonzero(keep)[0]
    n_rest = -(-len(rest_cols) // LANES) * LANES
    wr = jnp.pad(w_in[:, rest_cols], ((0, 0), (0, n_rest - len(rest_cols)))).astype(MXU_DTYPE)
    inv = 1.0 / (ROPE_THETA ** (np.arange(0, HEAD_DIM, 2, dtype=np.float32) / HEAD_DIM))
    lane = np.arange(LANES)
    ang = jnp.arange(s_len, dtype=jnp.float32)[:, None] * jnp.asarray(inv[lane % (HEAD_DIM // 2)])[None, :]
    sign = np.where((lane % HEAD_DIM) < HEAD_DIM // 2, -1.0, 1.0).astype(np.float32)
    cos_t, sin_t = jnp.cos(ang), jnp.sin(ang) * sign[None, :]
    tm = _pick_tm(s_len, 512)
    per_s = s_len // tm
    row = lambda n: pl.BlockSpec((tm, n), lambda i: (i, 0))
    tab = pl.BlockSpec((tm, LANES), lambda i: (i % per_s, 0))
    qshape = jax.ShapeDtypeStruct((t, mix), MXU_DTYPE)
    qa, qi, qb, rest = pl.pallas_call(
        _even_proj_kernel,
        out_shape=(qshape, qshape, qshape, jax.ShapeDtypeStruct((t, n_rest), jnp.float32)),
        grid=(t // tm,),
        in_specs=[row(d), pl.BlockSpec((3, d, mix), lambda i: (0, 0, 0)), pl.BlockSpec((d, n_rest), lambda i: (0, 0)),
                  tab, tab],
        out_specs=(row(mix), row(mix), row(mix), row(n_rest)),
        compiler_params=_cparams("parallel"),
        name="even_in_proj",
    )(x, wq, wr, cos_t, sin_t)
    return qa, qi, qb, rest[:, :len(rest_cols)]


def _online_update(s, drop, vt, m_ref, l_ref, acc_ref, idx=Ellipsis):
    if drop is not None:
        s = s + drop
    m_old = m_ref[idx]
    m_new = jnp.maximum(m_old, jnp.max(s, axis=0, keepdims=True))
    p = jnp.exp2(s - m_new)
    alpha = jnp.exp2(m_old - m_new)
    l_ref[idx] = alpha * l_ref[idx] + jnp.sum(p, axis=0, keepdims=True)
    acc_ref[idx] = alpha * acc_ref[idx] + _dot(vt, p.astype(vt.dtype))
    m_ref[idx] = m_new


def _drop(pred, reps=1):
    d = jnp.where(pred, 0.0, NEG)
    return d if reps == 1 else jnp.concatenate([d] * reps, axis=1)


def _init_state(m_ref, l_ref, acc_ref):
    m_ref[...] = jnp.full(m_ref.shape, NEG, jnp.float32)
    l_ref[...] = jnp.zeros(l_ref.shape, jnp.float32)
    acc_ref[...] = jnp.zeros(acc_ref.shape, jnp.float32)


def _normalized(l_ref, acc_ref, idx=Ellipsis):
    return acc_ref[idx] / l_ref[idx]


def _pipelined_tiles(lo, hi, produce, consume, consume_last=None):
    consume_last = consume_last or consume
    produce(lo, 0)

    def pair(j, carry):
        kt = lo + 2 * j
        produce(kt + 1, 1)
        consume(kt, 0)
        produce(kt + 2, 0)
        consume(kt + 1, 1)
        return carry

    n_body = hi - lo - 1
    lax.fori_loop(0, n_body // 2, pair, 0)

    @pl.when(n_body % 2 == 0)
    def _():
        consume_last(hi - 1, 0)

    @pl.when(n_body % 2 == 1)
    def _():
        produce(hi - 1, 1)
        consume(hi - 2, 0)
        consume_last(hi - 1, 1)


def _stack_heads(q, n_heads):
    return jnp.concatenate([q[:, h * HEAD_DIM:(h + 1) * HEAD_DIM] for h in range(n_heads)], axis=0)


def _store_heads(o_t, o_ref, n_heads, tq):
    for j in range(n_heads // 2):
        pair = jnp.concatenate([o_t[:, (2 * j) * tq:(2 * j + 1) * tq],
                                o_t[:, (2 * j + 1) * tq:(2 * j + 2) * tq]], axis=0)
        o_ref[0, :, 2 * j * HEAD_DIM:(2 * j + 2) * HEAD_DIM] = pair.T.astype(o_ref.dtype)


def _dsa_kernel(qi_ref, w_ref, ki_ref, qa_ref, ka_ref, vat_ref, o_ref,
                key_ref, m_ref, l_ref, acc_ref, sbuf_ref, *, k_sel, idx_scale):
    i = pl.program_id(1)
    tq = QBLOCK
    tk = KEY_TILE
    n_tiles = (i * tq + tq + tk - 1) // tk
    q0 = i * tq
    t_row = q0 + lax.broadcasted_iota(jnp.int32, (1, tq), 1)
    kk = lax.broadcasted_iota(jnp.int32, (tk, tq), 0)

    qi = _stack_heads(qi_ref[0], IDX_HEADS)
    w_row = w_ref[0, 0]

    def tile_start(kt):
        return pl.multiple_of(kt * tk, tk)

    def logits_tile(kt, slot):
        sbuf_ref[slot] = _dot_nt(ki_ref[0, pl.ds(tile_start(kt), tk), :], qi)

    def score_tile(kt, slot):
        k0 = tile_start(kt)
        z = jnp.maximum(sbuf_ref[slot], 0.0) * w_row
        sc = z[:, 0:tq]
        for h in range(1, IDX_HEADS):
            sc = sc + z[:, h * tq:(h + 1) * tq]
        sc = sc * idx_scale
        sc = jnp.where(k0 + kk <= t_row, sc, NEG)
        bits = lax.bitcast_convert_type(sc, jnp.int32)
        key_ref[pl.ds(k0, tk), :] = _order_key(bits)

    _pipelined_tiles(0, n_tiles, logits_tile, score_tile)

    @pl.when(n_tiles % 2 == 1)
    def _():
        key_ref[pl.ds(tile_start(n_tiles), tk), :] = jnp.full((tk, tq), NEG_KEY, jnp.int32)

    ct = 2 * tk
    kk2 = lax.broadcasted_iota(jnp.int32, (ct, tq), 0)

    def count(pred_fn):
        def body(kt, acc):
            k0 = pl.multiple_of(kt * ct, ct)
            c = pred_fn(key_ref[pl.ds(k0, ct), :], k0 + kk2).astype(jnp.int32)
            return acc + jnp.sum(c.reshape(ct // SUBLANES, SUBLANES, tq), axis=0)
        acc = lax.fori_loop(0, (n_tiles + 1) // 2, body, jnp.zeros((SUBLANES, tq), jnp.int32))
        return jnp.sum(acc, axis=0, keepdims=True)

    def search(_):
        c0 = count(lambda key, idx: key >= 0)
        thr = jnp.where(c0 >= k_sel, 0, INT_MIN).astype(jnp.int32)

        def bit_step(b, thr):
            trial = thr | (jnp.int32(1) << (30 - b))
            c = count(lambda key, idx: key >= trial)
            return jnp.where(c >= k_sel, trial, thr)

        thr = lax.fori_loop(0, 31, bit_step, thr)
        c_ge = count(lambda key, idx: key >= thr)

        def tie_search(_):
            c_gt = count(lambda key, idx: key > thr)
            need = k_sel - c_gt

            def idx_step(b, u):
                trial = u | (jnp.int32(1) << (20 - b))
                c = count(lambda key, idx: (key == thr) & (idx < trial))
                return jnp.where(c < need, trial, u)

            return lax.fori_loop(0, 21, idx_step, jnp.zeros((1, tq), jnp.int32))

        cut = lax.cond(jnp.max(c_ge) > k_sel, tie_search,
                       lambda _: jnp.full((1, tq), IDX_ALL, jnp.int32), 0)
        return thr, cut

    thr, cut = lax.cond(q0 + tq > k_sel, search,
                        lambda _: (jnp.full((1, tq), INT_MIN, jnp.int32),
                                   jnp.full((1, tq), IDX_ALL, jnp.int32)), 0)

    qa = _stack_heads(qa_ref[0], A_HEADS)
    _init_state(m_ref, l_ref, acc_ref)

    def qk_tile(kt, slot):
        sbuf_ref[slot] = _dot_nt(ka_ref[0, pl.ds(tile_start(kt), tk), :], qa)

    def attn_tile(kt, slot):
        k0 = tile_start(kt)
        key = key_ref[pl.ds(k0, tk), :]
        idx = k0 + kk
        sel = ((key > thr) | ((key == thr) & (idx <= cut))) & (idx <= t_row)
        _online_update(sbuf_ref[slot], _drop(sel, A_HEADS), vat_ref[0, kt], m_ref, l_ref, acc_ref)

    _pipelined_tiles(0, n_tiles, qk_tile, attn_tile)
    _store_heads(_normalized(l_ref, acc_ref), o_ref, A_HEADS, tq)


def _value_tiles_t(v, tk):
    b, s, d = v.shape
    return jnp.transpose(v.reshape(b, s // tk, tk, d), (0, 1, 3, 2)).astype(MXU_DTYPE)


def _per_query_rows(x, tq):
    b, s, h = x.shape
    return jnp.transpose(x.reshape(b, s // tq, tq, h), (0, 1, 3, 2)).reshape(b, s // tq, 1, h * tq)


def _dsa_attention(q_a, k_a, v_a, q_i, k_i, w_i):
    b, s, _ = q_a.shape
    tq = QBLOCK
    nq = s // tq
    k_sel = min(DSA_TOPK_MAX, s // 4)
    tk = KEY_TILE
    assert (s // tk) % 2 == 0
    kern = functools.partial(_dsa_kernel, k_sel=k_sel, idx_scale=float((IDX_HEADS * IDX_DIM) ** -0.5))
    hq = A_HEADS * tq
    return pl.pallas_call(
        kern,
        out_shape=jax.ShapeDtypeStruct((b, s, A_HEADS * HEAD_DIM), MXU_DTYPE),
        grid=(b, nq),
        in_specs=[
            pl.BlockSpec((1, tq, IDX_HEADS * IDX_DIM), lambda bi, i: (bi, i, 0)),
            pl.BlockSpec((1, 1, 1, hq), lambda bi, i: (bi, i, 0, 0)),
            pl.BlockSpec((1, s, IDX_DIM), lambda bi, i: (bi, 0, 0)),
            pl.BlockSpec((1, tq, A_HEADS * HEAD_DIM), lambda bi, i: (bi, i, 0)),
            pl.BlockSpec((1, s, HEAD_DIM), lambda bi, i: (bi, 0, 0)),
            pl.BlockSpec((1, s // tk, HEAD_DIM, tk), lambda bi, i: (bi, 0, 0, 0)),
        ],
        out_specs=pl.BlockSpec((1, tq, A_HEADS * HEAD_DIM), lambda bi, i: (bi, i, 0)),
        scratch_shapes=[pltpu.VMEM((s, tq), jnp.int32),
                        pltpu.VMEM((1, hq), jnp.float32),
                        pltpu.VMEM((1, hq), jnp.float32),
                        pltpu.VMEM((HEAD_DIM, hq), jnp.float32),
                        pltpu.VMEM((2, tk, hq), jnp.float32)],
        compiler_params=_cparams("parallel", "arbitrary"),
        name="dsa_attention",
    )(q_i, _per_query_rows(w_i.astype(jnp.float32), tq), k_i.astype(MXU_DTYPE), q_a, k_a.astype(MXU_DTYPE),
      _value_tiles_t(v_a, tk))


def _cmp_kernel(ck_ref, pe_ref, w1_ref, w2_ref, o_ref):
    nc = ck_ref.shape[1]
    half = ck_ref.shape[2]
    ck = ck_ref[0]
    a = _dot((ck + pe_ref[0:1, :]).astype(MXU_DTYPE), w1_ref[0:half, :])
    bm = _dot((ck + pe_ref[1:2, :]).astype(MXU_DTYPE), w1_ref[half:2 * half, :])
    h = a + pltpu.roll(bm, nc - 1, 0)
    o_ref[0] = _dot(_silu(h).astype(MXU_DTYPE), w2_ref[...])


def _compress(kv, pe, w1, w2):
    b, s, d = kv.shape
    nc = s // CMP_STRIDE
    half = CMP_STRIDE * d
    ck = kv.reshape(b, nc, half)
    pe2 = pe.reshape(2, half)
    return pl.pallas_call(
        _cmp_kernel,
        out_shape=jax.ShapeDtypeStruct((b, nc, d), jnp.float32),
        grid=(b,),
        in_specs=[pl.BlockSpec((1, nc, half), lambda bi: (bi, 0, 0)),
                  pl.BlockSpec((2, half), lambda bi: (0, 0)),
                  pl.BlockSpec((2 * half, d), lambda bi: (0, 0)),
                  pl.BlockSpec((d, d), lambda bi: (0, 0))],
        out_specs=pl.BlockSpec((1, nc, d), lambda bi: (bi, 0, 0)),
        compiler_params=_cparams("parallel"),
        name="nsa_compress",
    )(ck, pe2, w1.astype(MXU_DTYPE), w2.astype(MXU_DTYPE))


def _split3(x):
    def top(v):
        return lax.bitcast_convert_type(lax.bitcast_convert_type(v, jnp.int32) & jnp.int32(-65536), jnp.float32)
    hi = top(x)
    r1 = x - hi
    mid = top(r1)
    lo = r1 - mid
    return hi.astype(MXU_DTYPE), mid.astype(MXU_DTYPE), lo.astype(MXU_DTYPE)


def _nsa_kernel(q_ref, g_ref, kc_ref, vct_ref, ovt_ref, exp_ref, ks_ref, vst_ref, kw_ref, vwt_ref, o_ref,
                tok_ref, m_ref, l_ref, acc_ref, m2_ref, l2_ref, acc2_ref, out_ref, sbuf_ref, *, n_sel, n_cmp):
    i = pl.program_id(1)
    tq = QBLOCK
    tk = KEY_TILE
    nh = B_HEADS
    hq = nh * tq
    q0 = i * tq
    n_tiles = (q0 + tq + tk - 1) // tk
    t_row = q0 + lax.broadcasted_iota(jnp.int32, (1, tq), 1)
    kk = lax.broadcasted_iota(jnp.int32, (tk, tq), 0)
    q = _stack_heads(q_ref[0], nh)
    gates = g_ref[0, 0]

    nc = kc_ref.shape[1]
    c_id = lax.broadcasted_iota(jnp.int32, (nc, tq), 0)
    valid_c = (c_id * CMP_STRIDE + (CMP_LEN - 1) <= t_row) & (c_id < n_cmp)
    keep_c = jnp.concatenate([jnp.where(valid_c, 1.0, 0.0)] * nh, axis=1)
    s_c = _dot_nt(kc_ref[0], q) + _drop(valid_c, nh)
    e_c = jnp.exp2(s_c - jnp.max(s_c, axis=0, keepdims=True)) * keep_c
    den = jnp.sum(e_c, axis=0, keepdims=True)
    p_c = e_c / jnp.where(den > 0.0, den, 1.0)
    out_ref[...] = gates[0:1, :] * _dot(vct_ref[0], p_c.astype(MXU_DTYPE))

    p_sum = p_c[:, 0:tq]
    for h in range(1, nh):
        p_sum = p_sum + p_c[:, h * tq:(h + 1) * tq]
    ovt = ovt_ref[...]
    pieces = _split3(p_sum)
    imp = _dot(ovt, pieces[0]) + _dot(ovt, pieces[1]) + _dot(ovt, pieces[2])
    n_sb = ovt.shape[0]
    j_id = lax.broadcasted_iota(jnp.int32, (n_sb, tq), 0)
    cur = t_row >> (SEL_LEN.bit_length() - 1)
    forced = (j_id == 0) | (j_id == cur) | (j_id == cur - 1)
    future = j_id * SEL_LEN > t_row
    imp = jnp.where(forced, BIG, jnp.where(future, NEG, imp))
    chosen = jnp.zeros((n_sb, tq), jnp.bool_)
    for _ in range(n_sel):
        cand = jnp.where(chosen, -jnp.inf, imp)
        best = jnp.max(cand, axis=0, keepdims=True)
        first = jnp.min(jnp.where(cand == best, j_id, n_sb), axis=0, keepdims=True)
        chosen = chosen | (j_id == first)
    tok_ref[...] = _dot(exp_ref[...], jnp.where(chosen, 1.0, 0.0).astype(MXU_DTYPE))

    _init_state(m_ref, l_ref, acc_ref)
    _init_state(m2_ref, l2_ref, acc2_ref)
    win_lo = jnp.maximum(q0 - WINDOW, 0) // tk

    def tile_start(kt):
        return pl.multiple_of(kt * tk, tk)

    def sel_qk(kt, slot):
        sbuf_ref[slot] = _dot_nt(ks_ref[0, pl.ds(tile_start(kt), tk), :], q)

    def sel_tile(kt, slot):
        k0 = tile_start(kt)
        sel = (tok_ref[pl.ds(k0, tk), :] > 0.5) & (k0 + kk <= t_row)
        _online_update(sbuf_ref[slot], _drop(sel, nh), vst_ref[0, kt], m_ref, l_ref, acc_ref)

    def win_qk(kt, slot):
        sbuf_ref[slot] = _dot_nt(kw_ref[0, pl.ds(tile_start(kt), tk), :], q)

    def win_tile(kt, slot):
        kpos = tile_start(kt) + kk
        ok = (kpos <= t_row) & (kpos > t_row - WINDOW)
        _online_update(sbuf_ref[slot], _drop(ok, nh), vwt_ref[0, kt], m2_ref, l2_ref, acc2_ref)

    _pipelined_tiles(0, n_tiles, sel_qk, sel_tile)
    _pipelined_tiles(win_lo, n_tiles, win_qk, win_tile)
    o_t = out_ref[...] + gates[1:2, :] * _normalized(l_ref, acc_ref) + gates[2:3, :] * _normalized(l2_ref, acc2_ref)
    _store_heads(o_t, o_ref, nh, tq)


def _nsa_attention(q_b, k_c, v_c, k_s, v_s, k_w, v_w, gate):
    b, s, _ = q_b.shape
    tq = QBLOCK
    nq = s // tq
    nh = B_HEADS
    hq = nh * tq
    nc = k_c.shape[1]
    n_cmp = (s - CMP_LEN) // CMP_STRIDE + 1
    n_sb = s // SEL_LEN
    n_sel = min(SEL_BLOCKS_MAX, n_sb)
    c_start = np.arange(nc) * CMP_STRIDE
    s_start = np.arange(n_sb) * SEL_LEN
    ov = np.clip(np.minimum(c_start[:, None] + CMP_LEN, s_start[None, :] + SEL_LEN)
                 - np.maximum(c_start[:, None], s_start[None, :]), 0, None) / CMP_LEN
    ov[n_cmp:] = 0.0
    ovt = jnp.asarray(ov.T, dtype=MXU_DTYPE)
    expand = jnp.asarray((np.arange(s)[:, None] // SEL_LEN == np.arange(n_sb)[None, :]), dtype=MXU_DTYPE)
    g = _sigmoid(gate.astype(jnp.float32))
    g = jnp.transpose(g.reshape(b, nq, tq, nh, 3), (0, 1, 4, 3, 2)).reshape(b, nq, 3, hq)
    tk = KEY_TILE
    kern = functools.partial(_nsa_kernel, n_sel=n_sel, n_cmp=n_cmp)
    full = lambda shape: pl.BlockSpec(shape, lambda bi, i: (0,) * len(shape))
    per_b = lambda shape: pl.BlockSpec(shape, lambda bi, i: (bi,) + (0,) * (len(shape) - 1))
    state = [pltpu.VMEM((1, hq), jnp.float32), pltpu.VMEM((1, hq), jnp.float32),
             pltpu.VMEM((HEAD_DIM, hq), jnp.float32)]
    return pl.pallas_call(
        kern,
        out_shape=jax.ShapeDtypeStruct((b, s, nh * HEAD_DIM), MXU_DTYPE),
        grid=(b, nq),
        in_specs=[
            pl.BlockSpec((1, tq, nh * HEAD_DIM), lambda bi, i: (bi, i, 0)),
            pl.BlockSpec((1, 1, 3, hq), lambda bi, i: (bi, i, 0, 0)),
            per_b((1, nc, HEAD_DIM)),
            per_b((1, HEAD_DIM, nc)),
            full((n_sb, nc)),
            full((s, n_sb)),
            per_b((1, s, HEAD_DIM)),
            per_b((1, s // tk, HEAD_DIM, tk)),
            per_b((1, s, HEAD_DIM)),
            per_b((1, s // tk, HEAD_DIM, tk)),
        ],
        out_specs=pl.BlockSpec((1, tq, nh * HEAD_DIM), lambda bi, i: (bi, i, 0)),
        scratch_shapes=[pltpu.VMEM((s, tq), jnp.float32)] + state + state
                       + [pltpu.VMEM((HEAD_DIM, hq), jnp.float32), pltpu.VMEM((2, tk, hq), jnp.float32)],
        compiler_params=_cparams("parallel", "arbitrary"),
        name="nsa_attention",
    )(q_b, g, k_c.astype(MXU_DTYPE),
      jnp.transpose(v_c, (0, 2, 1)).astype(MXU_DTYPE), ovt, expand,
      k_s.astype(MXU_DTYPE), _value_tiles_t(v_s, tk), k_w.astype(MXU_DTYPE), _value_tiles_t(v_w, tk))


FOX_TQ = 256
FOX_HB = 4
FOX_KPAD = 128


def _fox_kernel(q_ref, k_ref, vt_ref, o_ref, m_ref, l_ref, acc_ref, sbuf_ref):
    i = pl.program_id(2)
    tq = FOX_TQ
    _init_state(m_ref, l_ref, acc_ref)

    def qk_tile(kt, slot):
        k0 = pl.multiple_of(kt * tq, tq)
        for hh in range(FOX_HB):
            sbuf_ref[slot, hh] = _dot_nt(k_ref[0, hh, pl.ds(k0, tq), :], q_ref[0, hh])

    def update(kt, slot, drop):
        for hh in range(FOX_HB):
            _online_update(sbuf_ref[slot, hh], drop, vt_ref[0, hh, kt], m_ref, l_ref, acc_ref, idx=hh)

    def diag_tile(kt, slot):
        kk = lax.broadcasted_iota(jnp.int32, (tq, tq), 0)
        qq = lax.broadcasted_iota(jnp.int32, (tq, tq), 1)
        update(kt, slot, _drop(kk <= qq))

    _pipelined_tiles(0, i + 1, qk_tile, lambda kt, slot: update(kt, slot, None), diag_tile)
    for j in range(FOX_HB // 2):
        pair = jnp.concatenate([_normalized(l_ref, acc_ref, 2 * j), _normalized(l_ref, acc_ref, 2 * j + 1)], axis=0)
        o_ref[0, :, 2 * j * HEAD_DIM:(2 * j + 2) * HEAD_DIM] = pair.T.astype(o_ref.dtype)


def _fox_attention(q, k, v, log_f):
    b, s, _ = q.shape
    nh = C_HEADS
    tq = min(FOX_TQ, s)
    assert tq == FOX_TQ
    d_cum = jnp.cumsum(log_f, axis=1)
    d3 = jnp.stack(_split3(d_cum * LOG2E), axis=-1)
    ones = jnp.ones((b, s, nh, 3), MXU_DTYPE)
    pad = jnp.zeros((b, s, nh, FOX_KPAD - HEAD_DIM - 3), MXU_DTYPE)
    qh = (q * ATTN_SCALE).reshape(b, s, nh, HEAD_DIM).astype(MXU_DTYPE)
    kh = k.reshape(b, s, nh, HEAD_DIM).astype(MXU_DTYPE)
    q_aug = jnp.transpose(jnp.concatenate([qh, ones, pad], axis=-1), (0, 2, 1, 3))
    k_aug = jnp.transpose(jnp.concatenate([kh, -d3, pad], axis=-1), (0, 2, 1, 3))
    vt = jnp.transpose(v.reshape(b, s // tq, tq, nh, HEAD_DIM), (0, 3, 1, 4, 2)).astype(MXU_DTYPE)
    return pl.pallas_call(
        _fox_kernel,
        out_shape=jax.ShapeDtypeStruct((b, s, nh * HEAD_DIM), MXU_DTYPE),
        grid=(b, nh // FOX_HB, s // tq),
        in_specs=[
            pl.BlockSpec((1, FOX_HB, tq, FOX_KPAD), lambda bi, j, i: (bi, j, i, 0)),
            pl.BlockSpec((1, FOX_HB, s, FOX_KPAD), lambda bi, j, i: (bi, j, 0, 0)),
            pl.BlockSpec((1, FOX_HB, s // tq, HEAD_DIM, tq), lambda bi, j, i: (bi, j, 0, 0, 0)),
        ],
        out_specs=pl.BlockSpec((1, tq, FOX_HB * HEAD_DIM), lambda bi, j, i: (bi, i, j)),
        scratch_shapes=[pltpu.VMEM((FOX_HB, 1, tq), jnp.float32),
                        pltpu.VMEM((FOX_HB, 1, tq), jnp.float32),
                        pltpu.VMEM((FOX_HB, HEAD_DIM, tq), jnp.float32),
                        pltpu.VMEM((2, FOX_HB, tq, tq), jnp.float32)],
        compiler_params=_cparams("parallel", "parallel", "arbitrary"),
        name="fox_attention",
    )(q_aug, k_aug, vt)


def _layer_norm(y, g, b):
    mu = jnp.mean(y, axis=-1, keepdims=True)
    d = y - mu
    var = jnp.mean(d * d, axis=-1, keepdims=True)
    return d * lax.rsqrt(var + LN_EPS) * g + b


def _mid_kernel(o_ref, x_ref, wo_ref, g_ref, b_ref, sw1_ref, sw3_ref, sw2_ref, rw_ref,
                x1_ref, x1b_ref, sh_ref, lg_ref, *, alpha):
    y = alpha * x_ref[...] + _dot(o_ref[...], wo_ref[...])
    x1 = _layer_norm(y, g_ref[...], b_ref[...])
    x1_ref[...] = x1
    xb = x1.astype(MXU_DTYPE)
    x1b_ref[...] = xb
    hid = _silu(_dot(xb, sw1_ref[...])) * _dot(xb, sw3_ref[...])
    sh_ref[...] = _dot(hid.astype(MXU_DTYPE), sw2_ref[...])
    lg_ref[...] = _dot_nt(rw_ref[...], xb)


def _mid_block(o, x, w_o, ln_g, ln_b, sw1, sw3, sw2, router_w, alpha):
    t, d = x.shape
    f = sw1.shape[1]
    e = router_w.shape[1]
    tm = _pick_tm(t, 512)
    row = lambda n: pl.BlockSpec((tm, n), lambda i: (i, 0))
    res = lambda a, c: pl.BlockSpec((a, c), lambda i: (0, 0))
    return pl.pallas_call(
        functools.partial(_mid_kernel, alpha=alpha),
        out_shape=(jax.ShapeDtypeStruct((t, d), jnp.float32),
                   jax.ShapeDtypeStruct((t, d), MXU_DTYPE),
                   jax.ShapeDtypeStruct((t, d), jnp.float32),
                   jax.ShapeDtypeStruct((e, t), jnp.float32)),
        grid=(t // tm,),
        in_specs=[row(o.shape[1]), row(d), res(o.shape[1], d), res(1, d), res(1, d),
                  res(d, f), res(d, f), res(f, d), res(e, d)],
        out_specs=(row(d), row(d), row(d), pl.BlockSpec((e, tm), lambda i: (0, i))),
        compiler_params=_cparams("parallel"),
        name="mixer_out_ln_shared_router",
    )(o, x, w_o.astype(MXU_DTYPE), ln_g.reshape(1, d), ln_b.reshape(1, d),
      sw1.astype(MXU_DTYPE), sw3.astype(MXU_DTYPE), sw2.astype(MXU_DTYPE), router_w.T.astype(MXU_DTYPE))


def _expert_kernel(blk_e_ref, xs_ref, w1_ref, w3_ref, w2_ref, y_ref):
    xe = xs_ref[...]
    hid = _silu(_dot(xe, w1_ref[0])) * _dot(xe, w3_ref[0])
    y_ref[...] = _dot(hid.astype(MXU_DTYPE), w2_ref[0]).astype(y_ref.dtype)


def _grouped_experts(xs, blk_e, w1, w3, w2):
    n_slots, d = xs.shape
    f = w1.shape[2]
    n_blocks = n_slots // EXPERT_BLOCK
    return pl.pallas_call(
        _expert_kernel,
        out_shape=jax.ShapeDtypeStruct((n_slots, d), MXU_DTYPE),
        grid_spec=pltpu.PrefetchScalarGridSpec(
            num_scalar_prefetch=1,
            grid=(n_blocks,),
            in_specs=[pl.BlockSpec((EXPERT_BLOCK, d), lambda i, be: (i, 0)),
                      pl.BlockSpec((1, d, f), lambda i, be: (be[i], 0, 0)),
                      pl.BlockSpec((1, d, f), lambda i, be: (be[i], 0, 0)),
                      pl.BlockSpec((1, f, d), lambda i, be: (be[i], 0, 0))],
            out_specs=pl.BlockSpec((EXPERT_BLOCK, d), lambda i, be: (i, 0)),
        ),
        compiler_params=_cparams("arbitrary"),
        name="routed_experts",
    )(blk_e, xs, w1.astype(MXU_DTYPE), w3.astype(MXU_DTYPE), w2.astype(MXU_DTYPE))


ROUTER_TM = 512
MOE_PARTS = 2


def _pick_rows(rows, row_id, n):
    out = jnp.zeros((n,) + rows[0].shape[1:], rows[0].dtype)
    for r in range(n):
        out = jnp.where(row_id == r, rows[r], out)
    return out


def _router_kernel(lg_ref, bias_ref, tri_ref, eidx_ref, gate_ref, rank_ref, cnt_ref, carry_ref):
    e, tm = lg_ref.shape
    per_group = e // N_GROUPS

    @pl.when(pl.program_id(0) == 0)
    def _():
        carry_ref[...] = jnp.zeros(carry_ref.shape, jnp.float32)

    s = _sigmoid(lg_ref[...])
    sb = s + bias_ref[...]
    neg_inf = -jnp.inf

    sub_id = lax.broadcasted_iota(jnp.int32, (per_group, tm), 0)
    g_rows = []
    for g in range(N_GROUPS):
        blk = sb[g * per_group:(g + 1) * per_group, :]
        m1 = jnp.max(blk, axis=0, keepdims=True)
        f1 = jnp.min(jnp.where(blk == m1, sub_id, per_group), axis=0, keepdims=True)
        m2 = jnp.max(jnp.where(sub_id == f1, neg_inf, blk), axis=0, keepdims=True)
        g_rows.append(m1 + m2)
    g_id = lax.broadcasted_iota(jnp.int32, (N_GROUPS, tm), 0)
    gscore = _pick_rows(g_rows, g_id, N_GROUPS)

    e_id = lax.broadcasted_iota(jnp.int32, (e, tm), 0)
    e_group = e_id // per_group if per_group & (per_group - 1) else e_id >> (per_group.bit_length() - 1)
    g_taken = jnp.zeros((N_GROUPS, tm), jnp.bool_)
    e_allowed = jnp.zeros((e, tm), jnp.bool_)
    for _ in range(TOPK_GROUPS):
        cand = jnp.where(g_taken, neg_inf, gscore)
        best = jnp.max(cand, axis=0, keepdims=True)
        first = jnp.min(jnp.where(cand == best, g_id, N_GROUPS), axis=0, keepdims=True)
        g_taken = g_taken | (g_id == first)
        e_allowed = e_allowed | (e_group == first)

    masked = jnp.where(e_allowed, sb, NEG)
    chosen = jnp.zeros((e, tm), jnp.bool_)
    id_rows, sel_rows = [], []
    for _ in range(TOP_K):
        cand = jnp.where(chosen, neg_inf, masked)
        best = jnp.max(cand, axis=0, keepdims=True)
        first = jnp.min(jnp.where(cand == best, e_id, e), axis=0, keepdims=True)
        hit = e_id == first
        chosen = chosen | hit
        id_rows.append(first)
        sel_rows.append(jnp.sum(jnp.where(hit, s, 0.0), axis=0, keepdims=True))
    k_id = lax.broadcasted_iota(jnp.int32, (TOP_K, tm), 0)
    eidx = _pick_rows(id_rows, k_id, TOP_K)
    sel = _pick_rows(sel_rows, k_id, TOP_K)
    eidx_ref[...] = eidx
    gate_ref[...] = sel / jnp.sum(sel, axis=0, keepdims=True) * ROUTED_SCALE

    chosen_f = jnp.where(chosen, 1.0, 0.0)
    incl = _dot(chosen_f.astype(MXU_DTYPE), tri_ref[...])
    rank_dense = carry_ref[...] + incl - chosen_f
    rank_rows = [jnp.sum(jnp.where(e_id == id_rows[r], rank_dense, 0.0), axis=0, keepdims=True)
                 for r in range(TOP_K)]
    rank_ref[...] = _pick_rows(rank_rows, k_id, TOP_K).astype(jnp.int32)
    carry_ref[...] = carry_ref[...] + jnp.sum(chosen_f, axis=1, keepdims=True)
    cnt_ref[...] = carry_ref[...]


def _route(logits_t, router_bias, part, n_parts):
    e, t_all = logits_t.shape
    t = t_all // n_parts
    tm = _pick_tm(t, ROUTER_TM)
    off = part * (t // tm)
    tri = jnp.asarray(np.triu(np.ones((tm, tm), np.float32)), dtype=MXU_DTYPE)
    kt = lambda dt: jax.ShapeDtypeStruct((TOP_K, t), dt)
    col = pl.BlockSpec((TOP_K, tm), lambda i: (0, i))
    eidx, gates, rank, cnt = pl.pallas_call(
        _router_kernel,
        out_shape=(kt(jnp.int32), kt(jnp.float32), kt(jnp.int32), jax.ShapeDtypeStruct((e, 1), jnp.float32)),
        grid=(t // tm,),
        in_specs=[pl.BlockSpec((e, tm), lambda i: (0, i + off)),
                  pl.BlockSpec((e, 1), lambda i: (0, 0)),
                  pl.BlockSpec((tm, tm), lambda i: (0, 0))],
        out_specs=(col, col, col, pl.BlockSpec((e, 1), lambda i: (0, 0))),
        scratch_shapes=[pltpu.VMEM((e, 1), jnp.float32)],
        compiler_params=_cparams("arbitrary"),
        name="moe_router",
    )(logits_t, router_bias.astype(jnp.float32).reshape(e, 1), tri)
    return eidx, gates, rank, cnt[:, 0].astype(jnp.int32)


def _moe_routed(x1b, eidx, rank, counts, w1, w3, w2, tok_offset):
    n_tok = eidx.shape[1]
    tk = n_tok * TOP_K
    n_blocks = (tk + N_EXPERTS * (EXPERT_BLOCK - 1)) // EXPERT_BLOCK + 1
    n_slots = n_blocks * EXPERT_BLOCK
    padded = (counts + EXPERT_BLOCK - 1) // EXPERT_BLOCK * EXPERT_BLOCK
    e_ids = jnp.arange(N_EXPERTS, dtype=jnp.int32)
    pad_end = jnp.sum(jnp.where(e_ids[None, :] <= e_ids[:, None], padded[None, :], 0), axis=1)
    start_pad = (pad_end - padded).astype(jnp.int32)
    start_of = jnp.sum(jnp.where(eidx[None] == e_ids[:, None, None], start_pad[:, None, None], 0), axis=0)
    dest = start_of + rank
    tok1 = jnp.broadcast_to(jnp.arange(n_tok, dtype=jnp.int32)[None, :] + 1, (TOP_K, n_tok))
    hit = jnp.zeros((n_slots,), jnp.int32).at[dest.reshape(tk)].add(tok1.reshape(tk), unique_indices=True)
    slot_tok = jnp.where(hit > 0, hit - 1, jnp.arange(n_slots, dtype=jnp.int32) % n_tok) + tok_offset
    blk_start = jnp.arange(n_blocks, dtype=jnp.int32) * EXPERT_BLOCK
    blk_e = jnp.minimum(jnp.sum((pad_end[None, :] <= blk_start[:, None]).astype(jnp.int32), axis=1),
                        N_EXPERTS - 1).astype(jnp.int32)
    y = _grouped_experts(x1b[slot_tok], blk_e, w1, w3, w2)
    return y, dest


def _post_kernel(x1_ref, sh_ref, yg_ref, gt_ref, p_ref, g_ref, b_ref, wp_ref, ng_ref, wg_ref, o_ref, *, alpha):
    gates = gt_ref[...]
    routed = yg_ref[0].astype(jnp.float32) * gates[:, 0:1]
    for k in range(1, TOP_K):
        routed = routed + yg_ref[k].astype(jnp.float32) * gates[:, k:k + 1]
    y = alpha * x1_ref[...] + (sh_ref[...] + routed)
    x2 = _layer_norm(y, g_ref[...], b_ref[...])
    e = _dot(p_ref[...].astype(MXU_DTYPE), wp_ref[...])
    e = e * lax.rsqrt(jnp.mean(e * e, axis=-1, keepdims=True) + RMS_EPS) * ng_ref[...]
    gate = _sigmoid(_dot(x2.astype(MXU_DTYPE), wg_ref[...]))
    o_ref[...] = x2 + gate * e


def _post_block(x1, shared, y_pairs, gates_t, p, ln_g, ln_b, w_proj, norm_g, w_gate, alpha, part, n_parts):
    t_all, d = x1.shape
    t = t_all // n_parts
    pd = p.shape[1]
    tm = _pick_tm(t, 256)
    off = part * (t // tm)
    row = lambda n: pl.BlockSpec((tm, n), lambda i: (i, 0))
    full_row = lambda n: pl.BlockSpec((tm, n), lambda i: (i + off, 0))
    res = lambda a, c: pl.BlockSpec((a, c), lambda i: (0, 0))
    return pl.pallas_call(
        functools.partial(_post_kernel, alpha=alpha),
        out_shape=jax.ShapeDtypeStruct((t, d), jnp.float32),
        grid=(t // tm,),
        in_specs=[full_row(d), full_row(d), pl.BlockSpec((TOP_K, tm, d), lambda i: (0, i, 0)), row(TOP_K),
                  full_row(pd), res(1, d), res(1, d), res(pd, d), res(1, d), res(d, d)],
        out_specs=row(d),
        compiler_params=_cparams("parallel"),
        name="moe_combine_ln_ple",
    )(x1, shared, y_pairs, gates_t, p, ln_g.reshape(1, d), ln_b.reshape(1, d), w_proj.astype(MXU_DTYPE),
      norm_g.reshape(1, d), w_gate.astype(MXU_DTYPE))


def _rope_tables(s_len):
    inv = 1.0 / (ROPE_THETA ** (np.arange(0, HEAD_DIM, 2, dtype=np.float32) / HEAD_DIM))
    ang = jnp.arange(s_len, dtype=jnp.float32)[:, None] * jnp.asarray(inv, dtype=jnp.float32)[None, :]
    return jnp.cos(ang), jnp.sin(ang)


def _rope(x, cos, sin):
    b, l, w = x.shape
    xh = x.reshape(b, l, w // HEAD_DIM, HEAD_DIM)
    half = HEAD_DIM // 2
    x1, x2 = xh[..., :half], xh[..., half:]
    c = cos[:, None, :]
    s = sin[:, None, :]
    return jnp.concatenate([x1 * c - x2 * s, x2 * c + x1 * s], axis=-1).reshape(b, l, w)


def _split_cols(h, sizes):
    out, c = [], 0
    for n in sizes:
        out.append(h[..., c:c + n])
        c += n
    return out


def _even_mixer(x, w_in, b_gate, pos_k, w1_k, w2_k, pos_v, w1_v, w2_v, cos, sin):
    b, s, d = x.shape
    mix = A_HEADS * HEAD_DIM
    q_a, q_i, q_b, rest = _even_projection(x.reshape(b * s, d), w_in, s)
    q_a, q_i, q_b = (q.reshape(b, s, mix) for q in (q_a, q_i, q_b))
    k_a, v_a, k_i, w_i, kv_b, g_b = _split_cols(rest.reshape(b, s, -1),
                                                (HEAD_DIM, HEAD_DIM, IDX_DIM, IDX_HEADS, 6 * HEAD_DIM, 3 * B_HEADS))
    o_a = _dsa_attention(q_a, _rope(k_a, cos, sin), v_a, q_i, _rope(k_i, cos, sin), w_i)
    k_c, v_c, k_s, v_s, k_w, v_w = _split_cols(kv_b, (HEAD_DIM,) * 6)
    nc = s // CMP_STRIDE
    c_last = jnp.minimum(jnp.arange(nc) * CMP_STRIDE + CMP_LEN - 1, s - 1)
    kc = _rope(_compress(k_c, pos_k, w1_k, w2_k), cos[c_last], sin[c_last])
    vc = _compress(v_c, pos_v, w1_v, w2_v)
    gate = (g_b + b_gate).reshape(b, s, B_HEADS, 3)
    o_b = _nsa_attention(q_b, kc, vc, _rope(k_s, cos, sin), v_s, _rope(k_w, cos, sin), v_w, gate)
    return jnp.concatenate([o_a, o_b], axis=-1).reshape(b * s, 2 * mix)


def _odd_mixer(x, w_in, b_f):
    b, s, d = x.shape
    mix = C_HEADS * HEAD_DIM
    h = _matmul(x.reshape(b * s, d), w_in, name="odd_in_proj").reshape(b, s, -1)
    q, k, v, f = _split_cols(h, (mix, mix, mix, C_HEADS))
    log_f = jax.nn.log_sigmoid((f + b_f).astype(jnp.float32))
    return _fox_attention(q, k, v, log_f).reshape(b * s, mix)


def kernel(x, p, ev_w_in, ev_b_gate, ev_cmp_pos_k, ev_cmp_w1_k, ev_cmp_w2_k, ev_cmp_pos_v, ev_cmp_w1_v, ev_cmp_w2_v, ev_w_o, od_w_in, od_b_f, od_w_o, ln1_g, ln1_b, ln2_g, ln2_b, router_w, router_bias, exp_w1, exp_w3, exp_w2, sh_w1, sh_w3, sh_w2, ple_w_gate, ple_w_proj, ple_norm_g):
    b, s, d = x.shape
    depth = p.shape[0]
    alpha = float((2.0 * depth) ** 0.25)
    cos, sin = _rope_tables(s)
    xf = x.reshape(b * s, d)
    for i in range(depth):
        j = i // 2
        x3 = xf.reshape(b, s, d)
        if i % 2 == 0:
            o = _even_mixer(x3, ev_w_in[j], ev_b_gate[j], ev_cmp_pos_k[j], ev_cmp_w1_k[j], ev_cmp_w2_k[j],
                            ev_cmp_pos_v[j], ev_cmp_w1_v[j], ev_cmp_w2_v[j], cos, sin)
            w_o = ev_w_o[j]
        else:
            o = _odd_mixer(x3, od_w_in[j], od_b_f[j])
            w_o = od_w_o[j]
        x1, x1b, shared, logits_t = _mid_block(o, xf, w_o, ln1_g[i], ln1_b[i], sh_w1[i], sh_w3[i], sh_w2[i],
                                               router_w[i], alpha)
        done = []
        for part in range(MOE_PARTS):
            eidx, gates, rank, counts = _route(logits_t, router_bias[i], part, MOE_PARTS)
            y, dest = _moe_routed(x1b, eidx, rank, counts, exp_w1[i], exp_w3[i], exp_w2[i],
                                  part * (b * s // MOE_PARTS))
            done.append(_post_block(x1, shared, y[dest], gates.T, p[i].reshape(b * s, -1), ln2_g[i], ln2_b[i],
                                    ple_w_proj[i], ple_norm_g[i], ple_w_gate[i], alpha, part, MOE_PARTS))
        xf = jnp.concatenate(done, axis=0)
    return xf.reshape(b, s, d)
```

```python
import functools

import numpy as np
import jax
import jax.numpy as jnp
from jax import lax
from jax.experimental import pallas as pl
from jax.experimental.pallas import tpu as pltpu

HEAD_DIM = 64
QBLOCK = 128
ROPE_THETA = 10000.0
LN_EPS = 1e-5
RMS_EPS = 1e-6
NEG = -1e30
BIG = 1e30

A_HEADS = 8
IDX_HEADS = 8
IDX_DIM = 64
DSA_TOPK_MAX = 256

B_HEADS = 8
CMP_LEN = 32
CMP_STRIDE = 16
SEL_LEN = 32
SEL_BLOCKS_MAX = 8
WINDOW = 512

C_HEADS = 16

N_EXPERTS = 64
TOP_K = 8
N_GROUPS = 8
TOPK_GROUPS = 4
ROUTED_SCALE = 2.5
EXPERT_BLOCK = 512

LANES = 128
SUBLANES = 8
VMEM_LIMIT = 48 * 1024 * 1024

MXU_DTYPE = jnp.bfloat16
KEY_TILE = 256
ATTN_SCALE = float(HEAD_DIM ** -0.5 * np.log2(np.e))
LOG2E = float(np.log2(np.e))
INT_MIN = -(2 ** 31)
IDX_ALL = 2 ** 30


def _order_key(bits):
    return bits ^ ((bits >> 31) & 0x7FFFFFFF)


NEG_KEY = int(_order_key(np.float32(NEG).view(np.int32).astype(np.int64)).astype(np.int32))


def _cparams(*sem):
    return pltpu.CompilerParams(dimension_semantics=sem, vmem_limit_bytes=VMEM_LIMIT)


def _dot(a, b):
    return jnp.dot(a, b, preferred_element_type=jnp.float32)


def _dot_nt(a, b):
    return lax.dot_general(a, b, (((1,), (1,)), ((), ())), preferred_element_type=jnp.float32)


def _silu(x):
    return x * (1.0 / (1.0 + jnp.exp(-x)))


def _sigmoid(x):
    return 1.0 / (1.0 + jnp.exp(-x))


def _mm_kernel(x_ref, w_ref, o_ref):
    o_ref[...] = _dot(x_ref[...].astype(MXU_DTYPE), w_ref[...])


def _pick_tm(t, cap):
    tm = min(cap, t)
    while t % tm:
        tm //= 2
    return tm


def _matmul(x, w, *, tm_cap=512, name="proj"):
    t, k = x.shape
    n = w.shape[1]
    n_pad = -(-n // LANES) * LANES
    w = jnp.pad(w, ((0, 0), (0, n_pad - n))).astype(MXU_DTYPE)
    tm = _pick_tm(t, tm_cap)
    out = pl.pallas_call(
        _mm_kernel,
        out_shape=jax.ShapeDtypeStruct((t, n_pad), jnp.float32),
        grid=(t // tm,),
        in_specs=[pl.BlockSpec((tm, k), lambda i: (i, 0)),
                  pl.BlockSpec((k, n_pad), lambda i: (0, 0))],
        out_specs=pl.BlockSpec((tm, n_pad), lambda i: (i, 0)),
        compiler_params=_cparams("parallel"),
        name=name,
    )(x, w)
    return out


def _rope_lanes(x, cos, sin_signed):
    lane = lax.broadcasted_iota(jnp.int32, (x.shape[0], LANES), 1)
    first_half = (lane % HEAD_DIM) < HEAD_DIM // 2
    out = []
    for c in range(x.shape[1] // LANES):
        blk = x[:, c * LANES:(c + 1) * LANES]
        partner = jnp.where(first_half, pltpu.roll(blk, LANES - HEAD_DIM // 2, 1), pltpu.roll(blk, HEAD_DIM // 2, 1))
        out.append(blk * cos + partner * sin_signed)
    return jnp.concatenate(out, axis=1)


def _even_proj_kernel(x_ref, wq_ref, wr_ref, cos_ref, sin_ref, qa_ref, qi_ref, qb_ref, rest_ref):
    xb = x_ref[...].astype(MXU_DTYPE)
    cos = cos_ref[...]
    sin = sin_ref[...]
    for g, (o_ref, scale) in enumerate(((qa_ref, ATTN_SCALE), (qi_ref, 1.0), (qb_ref, ATTN_SCALE))):
        roped = _rope_lanes(_dot(xb, wq_ref[g]), cos, sin)
        o_ref[...] = (roped * scale).astype(o_ref.dtype)
    rest_ref[...] = _dot(xb, wr_ref[...])


def _even_projection(x, w_in, s_len):
    t, d = x.shape
    mix = A_HEADS * HEAD_DIM
    q_cols = [(0, mix), (mix + 2 * HEAD_DIM, IDX_HEADS * IDX_DIM),
              (mix + 2 * HEAD_DIM + IDX_HEADS * IDX_DIM + IDX_DIM + IDX_HEADS, B_HEADS * HEAD_DIM)]
    wq = jnp.stack([w_in[:, c:c + n] for c, n in q_cols]).astype(MXU_DTYPE)
    keep = np.ones(w_in.shape[1], bool)
    for c, n in q_cols:
        keep[c:c + n] = False
    rest_cols = np.nonzero(keep)[0]
    n_rest = -(-len(rest_cols) // LANES) * LANES
    wr = jnp.pad(w_in[:, rest_cols], ((0, 0), (0, n_rest - len(rest_cols)))).astype(MXU_DTYPE)
    inv = 1.0 / (ROPE_THETA ** (np.arange(0, HEAD_DIM, 2, dtype=np.float32) / HEAD_DIM))
    lane = np.arange(LANES)
    ang = jnp.arange(s_len, dtype=jnp.float32)[:, None] * jnp.asarray(inv[lane % (HEAD_DIM // 2)])[None, :]
    sign = np.where((lane % HEAD_DIM) < HEAD_DIM // 2, -1.0, 1.0).astype(np.float32)
    cos_t, sin_t = jnp.cos(ang), jnp.sin(ang) * sign[None, :]
    tm = _pick_tm(s_len, 512)
    per_s = s_len // tm
    row = lambda n: pl.BlockSpec((tm, n), lambda i: (i, 0))
    tab = pl.BlockSpec((tm, LANES), lambda i: (i % per_s, 0))
    qshape = jax.ShapeDtypeStruct((t, mix), MXU_DTYPE)
    qa, qi, qb, rest = pl.pallas_call(
        _even_proj_kernel,
        out_shape=(qshape, qshape, qshape, jax.ShapeDtypeStruct((t, n_rest), jnp.float32)),
        grid=(t // tm,),
        in_specs=[row(d), pl.BlockSpec((3, d, mix), lambda i: (0, 0, 0)), pl.BlockSpec((d, n_rest), lambda i: (0, 0)),
                  tab, tab],
        out_specs=(row(mix), row(mix), row(mix), row(n_rest)),
        compiler_params=_cparams("parallel"),
        name="even_in_proj",
    )(x, wq, wr, cos_t, sin_t)
    return qa, qi, qb, rest[:, :len(rest_cols)]


def _online_update(s, drop, vt, m_ref, l_ref, acc_ref, idx=Ellipsis):
    if drop is not None:
        s = s + drop
    m_old = m_ref[idx]
    m_new = jnp.maximum(m_old, jnp.max(s, axis=0, keepdims=True))
    p = jnp.exp2(s - m_new)
    alpha = jnp.exp2(m_old - m_new)
    l_ref[idx] = alpha * l_ref[idx] + jnp.sum(p, axis=0, keepdims=True)
    acc_ref[idx] = alpha * acc_ref[idx] + _dot(vt, p.astype(vt.dtype))
    m_ref[idx] = m_new


def _drop(pred, reps=1):
    d = jnp.where(pred, 0.0, NEG)
    return d if reps == 1 else jnp.concatenate([d] * reps, axis=1)


def _init_state(m_ref, l_ref, acc_ref):
    m_ref[...] = jnp.full(m_ref.shape, NEG, jnp.float32)
    l_ref[...] = jnp.zeros(l_ref.shape, jnp.float32)
    acc_ref[...] = jnp.zeros(acc_ref.shape, jnp.float32)


def _normalized(l_ref, acc_ref, idx=Ellipsis):
    return acc_ref[idx] / l_ref[idx]


def _pipelined_tiles(lo, hi, produce, consume, consume_last=None):
    consume_last = consume_last or consume
    produce(lo, 0)

    def pair(j, carry):
        kt = lo + 2 * j
        produce(kt + 1, 1)
        consume(kt, 0)
        produce(kt + 2, 0)
        consume(kt + 1, 1)
        return carry

    n_body = hi - lo - 1
    lax.fori_loop(0, n_body // 2, pair, 0)

    @pl.when(n_body % 2 == 0)
    def _():
        consume_last(hi - 1, 0)

    @pl.when(n_body % 2 == 1)
    def _():
        produce(hi - 1, 1)
        consume(hi - 2, 0)
        consume_last(hi - 1, 1)


def _stack_heads(q, n_heads):
    return jnp.concatenate([q[:, h * HEAD_DIM:(h + 1) * HEAD_DIM] for h in range(n_heads)], axis=0)


def _store_heads(o_t, o_ref, n_heads, tq):
    for j in range(n_heads // 2):
        pair = jnp.concatenate([o_t[:, (2 * j) * tq:(2 * j + 1) * tq],
                                o_t[:, (2 * j + 1) * tq:(2 * j + 2) * tq]], axis=0)
        o_ref[0, :, 2 * j * HEAD_DIM:(2 * j + 2) * HEAD_DIM] = pair.T.astype(o_ref.dtype)


def _dsa_kernel(qi_ref, w_ref, ki_ref, qa_ref, ka_ref, vat_ref, o_ref,
                key_ref, m_ref, l_ref, acc_ref, sbuf_ref, *, k_sel, idx_scale):
    i = pl.program_id(1)
    tq = QBLOCK
    tk = KEY_TILE
    n_tiles = (i * tq + tq + tk - 1) // tk
    q0 = i * tq
    t_row = q0 + lax.broadcasted_iota(jnp.int32, (1, tq), 1)
    kk = lax.broadcasted_iota(jnp.int32, (tk, tq), 0)

    qi = _stack_heads(qi_ref[0], IDX_HEADS)
    w_row = w_ref[0, 0]

    def tile_start(kt):
        return pl.multiple_of(kt * tk, tk)

    def logits_tile(kt, slot):
        sbuf_ref[slot] = _dot_nt(ki_ref[0, pl.ds(tile_start(kt), tk), :], qi)

    def score_tile(kt, slot):
        k0 = tile_start(kt)
        z = jnp.maximum(sbuf_ref[slot], 0.0) * w_row
        sc = z[:, 0:tq]
        for h in range(1, IDX_HEADS):
            sc = sc + z[:, h * tq:(h + 1) * tq]
        sc = sc * idx_scale
        sc = jnp.where(k0 + kk <= t_row, sc, NEG)
        bits = lax.bitcast_convert_type(sc, jnp.int32)
        key_ref[pl.ds(k0, tk), :] = _order_key(bits)

    _pipelined_tiles(0, n_tiles, logits_tile, score_tile)

    @pl.when(n_tiles % 2 == 1)
    def _():
        key_ref[pl.ds(tile_start(n_tiles), tk), :] = jnp.full((tk, tq), NEG_KEY, jnp.int32)

    ct = 2 * tk
    kk2 = lax.broadcasted_iota(jnp.int32, (ct, tq), 0)

    def count(pred_fn):
        def body(kt, acc):
            k0 = pl.multiple_of(kt * ct, ct)
            c = pred_fn(key_ref[pl.ds(k0, ct), :], k0 + kk2).astype(jnp.int32)
            return acc + jnp.sum(c.reshape(ct // SUBLANES, SUBLANES, tq), axis=0)
        acc = lax.fori_loop(0, (n_tiles + 1) // 2, body, jnp.zeros((SUBLANES, tq), jnp.int32))
        return jnp.sum(acc, axis=0, keepdims=True)

    def search(_):
        c0 = count(lambda key, idx: key >= 0)
        thr = jnp.where(c0 >= k_sel, 0, INT_MIN).astype(jnp.int32)

        def bit_step(b, thr):
            trial = thr | (jnp.int32(1) << (30 - b))
            c = count(lambda key, idx: key >= trial)
            return jnp.where(c >= k_sel, trial, thr)

        thr = lax.fori_loop(0, 31, bit_step, thr)
        c_ge = count(lambda key, idx: key >= thr)

        def tie_search(_):
            c_gt = count(lambda key, idx: key > thr)
            need = k_sel - c_gt

            def idx_step(b, u):
                trial = u | (jnp.int32(1) << (20 - b))
                c = count(lambda key, idx: (key == thr) & (idx < trial))
                return jnp.where(c < need, trial, u)

            return lax.fori_loop(0, 21, idx_step, jnp.zeros((1, tq), jnp.int32))

        cut = lax.cond(jnp.max(c_ge) > k_sel, tie_search,
                       lambda _: jnp.full((1, tq), IDX_ALL, jnp.int32), 0)
        return thr, cut

    thr, cut = lax.cond(q0 + tq > k_sel, search,
                        lambda _: (jnp.full((1, tq), INT_MIN, jnp.int32),
                                   jnp.full((1, tq), IDX_ALL, jnp.int32)), 0)

    qa = _stack_heads(qa_ref[0], A_HEADS)
    _init_state(m_ref, l_ref, acc_ref)

    def qk_tile(kt, slot):
        sbuf_ref[slot] = _dot_nt(ka_ref[0, pl.ds(tile_start(kt), tk), :], qa)

    def attn_tile(kt, slot):
        k0 = tile_start(kt)
        key = key_ref[pl.ds(k0, tk), :]
        idx = k0 + kk
        sel = ((key > thr) | ((key == thr) & (idx <= cut))) & (idx <= t_row)
        _online_update(sbuf_ref[slot], _drop(sel, A_HEADS), vat_ref[0, kt], m_ref, l_ref, acc_ref)

    _pipelined_tiles(0, n_tiles, qk_tile, attn_tile)
    _store_heads(_normalized(l_ref, acc_ref), o_ref, A_HEADS, tq)


def _value_tiles_t(v, tk):
    b, s, d = v.shape
    return jnp.transpose(v.reshape(b, s // tk, tk, d), (0, 1, 3, 2)).astype(MXU_DTYPE)


def _per_query_rows(x, tq):
    b, s, h = x.shape
    return jnp.transpose(x.reshape(b, s // tq, tq, h), (0, 1, 3, 2)).reshape(b, s // tq, 1, h * tq)


def _dsa_attention(q_a, k_a, v_a, q_i, k_i, w_i):
    b, s, _ = q_a.shape
    tq = QBLOCK
    nq = s // tq
    k_sel = min(DSA_TOPK_MAX, s // 4)
    tk = KEY_TILE
    assert (s // tk) % 2 == 0
    kern = functools.partial(_dsa_kernel, k_sel=k_sel, idx_scale=float((IDX_HEADS * IDX_DIM) ** -0.5))
    hq = A_HEADS * tq
    return pl.pallas_call(
        kern,
        out_shape=jax.ShapeDtypeStruct((b, s, A_HEADS * HEAD_DIM), MXU_DTYPE),
        grid=(b, nq),
        in_specs=[
            pl.BlockSpec((1, tq, IDX_HEADS * IDX_DIM), lambda bi, i: (bi, i, 0)),
            pl.BlockSpec((1, 1, 1, hq), lambda bi, i: (bi, i, 0, 0)),
            pl.BlockSpec((1, s, IDX_DIM), lambda bi, i: (bi, 0, 0)),
            pl.BlockSpec((1, tq, A_HEADS * HEAD_DIM), lambda bi, i: (bi, i, 0)),
            pl.BlockSpec((1, s, HEAD_DIM), lambda bi, i: (bi, 0, 0)),
            pl.BlockSpec((1, s // tk, HEAD_DIM, tk), lambda bi, i: (bi, 0, 0, 0)),
        ],
        out_specs=pl.BlockSpec((1, tq, A_HEADS * HEAD_DIM), lambda bi, i: (bi, i, 0)),
        scratch_shapes=[pltpu.VMEM((s, tq), jnp.int32),
                        pltpu.VMEM((1, hq), jnp.float32),
                        pltpu.VMEM((1, hq), jnp.float32),
                        pltpu.VMEM((HEAD_DIM, hq), jnp.float32),
                        pltpu.VMEM((2, tk, hq), jnp.float32)],
        compiler_params=_cparams("parallel", "arbitrary"),
        name="dsa_attention",
    )(q_i, _per_query_rows(w_i.astype(jnp.float32), tq), k_i.astype(MXU_DTYPE), q_a, k_a.astype(MXU_DTYPE),
      _value_tiles_t(v_a, tk))


def _cmp_kernel(ck_ref, pe_ref, w1_ref, w2_ref, o_ref):
    nc = ck_ref.shape[1]
    half = ck_ref.shape[2]
    ck = ck_ref[0]
    a = _dot((ck + pe_ref[0:1, :]).astype(MXU_DTYPE), w1_ref[0:half, :])
    bm = _dot((ck + pe_ref[1:2, :]).astype(MXU_DTYPE), w1_ref[half:2 * half, :])
    h = a + pltpu.roll(bm, nc - 1, 0)
    o_ref[0] = _dot(_silu(h).astype(MXU_DTYPE), w2_ref[...])


def _compress(kv, pe, w1, w2):
    b, s, d = kv.shape
    nc = s // CMP_STRIDE
    half = CMP_STRIDE * d
    ck = kv.reshape(b, nc, half)
    pe2 = pe.reshape(2, half)
    return pl.pallas_call(
        _cmp_kernel,
        out_shape=jax.ShapeDtypeStruct((b, nc, d), jnp.float32),
        grid=(b,),
        in_specs=[pl.BlockSpec((1, nc, half), lambda bi: (bi, 0, 0)),
                  pl.BlockSpec((2, half), lambda bi: (0, 0)),
                  pl.BlockSpec((2 * half, d), lambda bi: (0, 0)),
                  pl.BlockSpec((d, d), lambda bi: (0, 0))],
        out_specs=pl.BlockSpec((1, nc, d), lambda bi: (bi, 0, 0)),
        compiler_params=_cparams("parallel"),
        name="nsa_compress",
    )(ck, pe2, w1.astype(MXU_DTYPE), w2.astype(MXU_DTYPE))


def _split3(x):
    def top(v):
        return lax.bitcast_convert_type(lax.bitcast_convert_type(v, jnp.int32) & jnp.int32(-65536), jnp.float32)
    hi = top(x)
    r1 = x - hi
    mid = top(r1)
    lo = r1 - mid
    return hi.astype(MXU_DTYPE), mid.astype(MXU_DTYPE), lo.astype(MXU_DTYPE)


def _nsa_kernel(q_ref, g_ref, kc_ref, vct_ref, ovt_ref, exp_ref, ks_ref, vst_ref, kw_ref, vwt_ref, o_ref,
                tok_ref, m_ref, l_ref, acc_ref, m2_ref, l2_ref, acc2_ref, out_ref, sbuf_ref, *, n_sel, n_cmp):
    i = pl.program_id(1)
    tq = QBLOCK
    tk = KEY_TILE
    nh = B_HEADS
    hq = nh * tq
    q0 = i * tq
    n_tiles = (q0 + tq + tk - 1) // tk
    t_row = q0 + lax.broadcasted_iota(jnp.int32, (1, tq), 1)
    kk = lax.broadcasted_iota(jnp.int32, (tk, tq), 0)
    q = _stack_heads(q_ref[0], nh)
    gates = g_ref[0, 0]

    nc = kc_ref.shape[1]
    c_id = lax.broadcasted_iota(jnp.int32, (nc, tq), 0)
    valid_c = (c_id * CMP_STRIDE + (CMP_LEN - 1) <= t_row) & (c_id < n_cmp)
    keep_c = jnp.concatenate([jnp.where(valid_c, 1.0, 0.0)] * nh, axis=1)
    s_c = _dot_nt(kc_ref[0], q) + _drop(valid_c, nh)
    e_c = jnp.exp2(s_c - jnp.max(s_c, axis=0, keepdims=True)) * keep_c
    den = jnp.sum(e_c, axis=0, keepdims=True)
    p_c = e_c / jnp.where(den > 0.0, den, 1.0)
    out_ref[...] = gates[0:1, :] * _dot(vct_ref[0], p_c.astype(MXU_DTYPE))

    p_sum = p_c[:, 0:tq]
    for h in range(1, nh):
        p_sum = p_sum + p_c[:, h * tq:(h + 1) * tq]
    ovt = ovt_ref[...]
    pieces = _split3(p_sum)
    imp = _dot(ovt, pieces[0]) + _dot(ovt, pieces[1]) + _dot(ovt, pieces[2])
    n_sb = ovt.shape[0]
    j_id = lax.broadcasted_iota(jnp.int32, (n_sb, tq), 0)
    cur = t_row >> (SEL_LEN.bit_length() - 1)
    forced = (j_id == 0) | (j_id == cur) | (j_id == cur - 1)
    future = j_id * SEL_LEN > t_row
    imp = jnp.where(forced, BIG, jnp.where(future, NEG, imp))
    chosen = jnp.zeros((n_sb, tq), jnp.bool_)
    for _ in range(n_sel):
        cand = jnp.where(chosen, -jnp.inf, imp)
        best = jnp.max(cand, axis=0, keepdims=True)
        first = jnp.min(jnp.where(cand == best, j_id, n_sb), axis=0, keepdims=True)
        chosen = chosen | (j_id == first)
    tok_ref[...] = _dot(exp_ref[...], jnp.where(chosen, 1.0, 0.0).astype(MXU_DTYPE))

    _init_state(m_ref, l_ref, acc_ref)
    _init_state(m2_ref, l2_ref, acc2_ref)
    win_lo = jnp.maximum(q0 - WINDOW, 0) // tk

    def tile_start(kt):
        return pl.multiple_of(kt * tk, tk)

    def sel_qk(kt, slot):
        sbuf_ref[slot] = _dot_nt(ks_ref[0, pl.ds(tile_start(kt), tk), :], q)

    def sel_tile(kt, slot):
        k0 = tile_start(kt)
        sel = (tok_ref[pl.ds(k0, tk), :] > 0.5) & (k0 + kk <= t_row)
        _online_update(sbuf_ref[slot], _drop(sel, nh), vst_ref[0, kt], m_ref, l_ref, acc_ref)

    def win_qk(kt, slot):
        sbuf_ref[slot] = _dot_nt(kw_ref[0, pl.ds(tile_start(kt), tk), :], q)

    def win_tile(kt, slot):
        kpos = tile_start(kt) + kk
        ok = (kpos <= t_row) & (kpos > t_row - WINDOW)
        _online_update(sbuf_ref[slot], _drop(ok, nh), vwt_ref[0, kt], m2_ref, l2_ref, acc2_ref)

    _pipelined_tiles(0, n_tiles, sel_qk, sel_tile)
    _pipelined_tiles(win_lo, n_tiles, win_qk, win_tile)
    o_t = out_ref[...] + gates[1:2, :] * _normalized(l_ref, acc_ref) + gates[2:3, :] * _normalized(l2_ref, acc2_ref)
    _store_heads(o_t, o_ref, nh, tq)


def _nsa_attention(q_b, k_c, v_c, k_s, v_s, k_w, v_w, gate):
    b, s, _ = q_b.shape
    tq = QBLOCK
    nq = s // tq
    nh = B_HEADS
    hq = nh * tq
    nc = k_c.shape[1]
    n_cmp = (s - CMP_LEN) // CMP_STRIDE + 1
    n_sb = s // SEL_LEN
    n_sel = min(SEL_BLOCKS_MAX, n_sb)
    c_start = np.arange(nc) * CMP_STRIDE
    s_start = np.arange(n_sb) * SEL_LEN
    ov = np.clip(np.minimum(c_start[:, None] + CMP_LEN, s_start[None, :] + SEL_LEN)
                 - np.maximum(c_start[:, None], s_start[None, :]), 0, None) / CMP_LEN
    ov[n_cmp:] = 0.0
    ovt = jnp.asarray(ov.T, dtype=MXU_DTYPE)
    expand = jnp.asarray((np.arange(s)[:, None] // SEL_LEN == np.arange(n_sb)[None, :]), dtype=MXU_DTYPE)
    g = _sigmoid(gate.astype(jnp.float32))
    g = jnp.transpose(g.reshape(b, nq, tq, nh, 3), (0, 1, 4, 3, 2)).reshape(b, nq, 3, hq)
    tk = KEY_TILE
    kern = functools.partial(_nsa_kernel, n_sel=n_sel, n_cmp=n_cmp)
    full = lambda shape: pl.BlockSpec(shape, lambda bi, i: (0,) * len(shape))
    per_b = lambda shape: pl.BlockSpec(shape, lambda bi, i: (bi,) + (0,) * (len(shape) - 1))
    state = [pltpu.VMEM((1, hq), jnp.float32), pltpu.VMEM((1, hq), jnp.float32),
             pltpu.VMEM((HEAD_DIM, hq), jnp.float32)]
    return pl.pallas_call(
        kern,
        out_shape=jax.ShapeDtypeStruct((b, s, nh * HEAD_DIM), MXU_DTYPE),
        grid=(b, nq),
        in_specs=[
            pl.BlockSpec((1, tq, nh * HEAD_DIM), lambda bi, i: (bi, i, 0)),
            pl.BlockSpec((1, 1, 3, hq), lambda bi, i: (bi, i, 0, 0)),
            per_b((1, nc, HEAD_DIM)),
            per_b((1, HEAD_DIM, nc)),
            full((n_sb, nc)),
            full((s, n_sb)),
            per_b((1, s, HEAD_DIM)),
            per_b((1, s // tk, HEAD_DIM, tk)),
            per_b((1, s, HEAD_DIM)),
            per_b((1, s // tk, HEAD_DIM, tk)),
        ],
        out_specs=pl.BlockSpec((1, tq, nh * HEAD_DIM), lambda bi, i: (bi, i, 0)),
        scratch_shapes=[pltpu.VMEM((s, tq), jnp.float32)] + state + state
                       + [pltpu.VMEM((HEAD_DIM, hq), jnp.float32), pltpu.VMEM((2, tk, hq), jnp.float32)],
        compiler_params=_cparams("parallel", "arbitrary"),
        name="nsa_attention",
    )(q_b, g, k_c.astype(MXU_DTYPE),
      jnp.transpose(v_c, (0, 2, 1)).astype(MXU_DTYPE), ovt, expand,
      k_s.astype(MXU_DTYPE), _value_tiles_t(v_s, tk), k_w.astype(MXU_DTYPE), _value_tiles_t(v_w, tk))


FOX_TQ = 256
FOX_HB = 8
FOX_KPAD = 128


def _fox_kernel(q_ref, k_ref, v_ref, d3_ref, o_ref, m_ref, l_ref, acc_ref, sbuf_ref, qaug_ref, kaug_ref, vt_ref):
    i = pl.program_id(2)
    tq = FOX_TQ
    s_len = k_ref.shape[1]

    @pl.when(i == 0)
    def _():
        for hh in range(FOX_HB):
            k_h = k_ref[0, :, hh * HEAD_DIM:(hh + 1) * HEAD_DIM]
            kaug_ref[hh] = jnp.concatenate([k_h, d3_ref[0, hh].astype(jnp.float32)], axis=1).astype(kaug_ref.dtype)
        for c in range(s_len // tq):
            for pr in range(FOX_HB // 2):
                blk_t = v_ref[0, c * tq:(c + 1) * tq, 2 * pr * HEAD_DIM:(2 * pr + 2) * HEAD_DIM].T
                vt_ref[2 * pr, c] = blk_t[0:HEAD_DIM].astype(vt_ref.dtype)
                vt_ref[2 * pr + 1, c] = blk_t[HEAD_DIM:2 * HEAD_DIM].astype(vt_ref.dtype)

    lane = lax.broadcasted_iota(jnp.int32, (tq, HEAD_DIM), 1)
    ones_cols = jnp.where(lane < 3, 1.0, 0.0)
    for hh in range(FOX_HB):
        q_h = q_ref[0, :, hh * HEAD_DIM:(hh + 1) * HEAD_DIM] * ATTN_SCALE
        qaug_ref[hh] = jnp.concatenate([q_h, ones_cols], axis=1).astype(qaug_ref.dtype)
    _init_state(m_ref, l_ref, acc_ref)

    def qk_tile(kt, slot):
        k0 = pl.multiple_of(kt * tq, tq)
        for hh in range(FOX_HB):
            sbuf_ref[slot, hh] = _dot_nt(kaug_ref[hh, pl.ds(k0, tq), :], qaug_ref[hh])

    def update(kt, slot, drop):
        for hh in range(FOX_HB):
            _online_update(sbuf_ref[slot, hh], drop, vt_ref[hh, kt], m_ref, l_ref, acc_ref, idx=hh)

    def diag_tile(kt, slot):
        kk = lax.broadcasted_iota(jnp.int32, (tq, tq), 0)
        qq = lax.broadcasted_iota(jnp.int32, (tq, tq), 1)
        update(kt, slot, _drop(kk <= qq))

    _pipelined_tiles(0, i + 1, qk_tile, lambda kt, slot: update(kt, slot, None), diag_tile)
    for j in range(FOX_HB // 2):
        pair = jnp.concatenate([_normalized(l_ref, acc_ref, 2 * j), _normalized(l_ref, acc_ref, 2 * j + 1)], axis=0)
        o_ref[0, :, 2 * j * HEAD_DIM:(2 * j + 2) * HEAD_DIM] = pair.T.astype(o_ref.dtype)


def _fox_attention(h, log_f):
    b, s, _ = h.shape
    nh = C_HEADS
    tq = FOX_TQ
    assert s % tq == 0
    gw = FOX_HB * HEAD_DIM
    ng = nh // FOX_HB
    d_cum = jnp.cumsum(log_f, axis=1)
    d3 = jnp.stack(_split3(d_cum * LOG2E), axis=-1)
    d3 = jnp.pad(jnp.transpose(-d3, (0, 2, 1, 3)), ((0, 0), (0, 0), (0, 0), (0, HEAD_DIM - 3)))
    return pl.pallas_call(
        _fox_kernel,
        out_shape=jax.ShapeDtypeStruct((b, s, nh * HEAD_DIM), MXU_DTYPE),
        grid=(b, ng, s // tq),
        in_specs=[
            pl.BlockSpec((1, tq, gw), lambda bi, j, i: (bi, i, j)),
            pl.BlockSpec((1, s, gw), lambda bi, j, i: (bi, 0, ng + j)),
            pl.BlockSpec((1, s, gw), lambda bi, j, i: (bi, 0, 2 * ng + j)),
            pl.BlockSpec((1, FOX_HB, s, HEAD_DIM), lambda bi, j, i: (bi, j, 0, 0)),
        ],
        out_specs=pl.BlockSpec((1, tq, gw), lambda bi, j, i: (bi, i, j)),
        scratch_shapes=[pltpu.VMEM((FOX_HB, 1, tq), jnp.float32),
                        pltpu.VMEM((FOX_HB, 1, tq), jnp.float32),
                        pltpu.VMEM((FOX_HB, HEAD_DIM, tq), jnp.float32),
                        pltpu.VMEM((2, FOX_HB, tq, tq), jnp.float32),
                        pltpu.VMEM((FOX_HB, tq, FOX_KPAD), MXU_DTYPE),
                        pltpu.VMEM((FOX_HB, s, FOX_KPAD), MXU_DTYPE),
                        pltpu.VMEM((FOX_HB, s // tq, HEAD_DIM, tq), MXU_DTYPE)],
        compiler_params=_cparams("parallel", "arbitrary", "arbitrary"),
        name="fox_attention",
    )(h, h, h, d3)


def _layer_norm(y, g, b):
    mu = jnp.mean(y, axis=-1, keepdims=True)
    d = y - mu
    var = jnp.mean(d * d, axis=-1, keepdims=True)
    return d * lax.rsqrt(var + LN_EPS) * g + b


def _mid_kernel(o_ref, x_ref, wo_ref, g_ref, b_ref, sw1_ref, sw3_ref, sw2_ref, rw_ref,
                x1_ref, x1b_ref, sh_ref, lg_ref, *, alpha):
    y = alpha * x_ref[...] + _dot(o_ref[...], wo_ref[...])
    x1 = _layer_norm(y, g_ref[...], b_ref[...])
    x1_ref[...] = x1
    xb = x1.astype(MXU_DTYPE)
    x1b_ref[...] = xb
    hid = _silu(_dot(xb, sw1_ref[...])) * _dot(xb, sw3_ref[...])
    sh_ref[...] = _dot(hid.astype(MXU_DTYPE), sw2_ref[...])
    lg_ref[...] = _dot_nt(rw_ref[...], xb)


def _mid_block(o, x, w_o, ln_g, ln_b, sw1, sw3, sw2, router_w, alpha):
    t, d = x.shape
    f = sw1.shape[1]
    e = router_w.shape[1]
    tm = _pick_tm(t, 512)
    row = lambda n: pl.BlockSpec((tm, n), lambda i: (i, 0))
    res = lambda a, c: pl.BlockSpec((a, c), lambda i: (0, 0))
    return pl.pallas_call(
        functools.partial(_mid_kernel, alpha=alpha),
        out_shape=(jax.ShapeDtypeStruct((t, d), jnp.float32),
                   jax.ShapeDtypeStruct((t, d), MXU_DTYPE),
                   jax.ShapeDtypeStruct((t, d), jnp.float32),
                   jax.ShapeDtypeStruct((e, t), jnp.float32)),
        grid=(t // tm,),
        in_specs=[row(o.shape[1]), row(d), res(o.shape[1], d), res(1, d), res(1, d),
                  res(d, f), res(d, f), res(f, d), res(e, d)],
        out_specs=(row(d), row(d), row(d), pl.BlockSpec((e, tm), lambda i: (0, i))),
        compiler_params=_cparams("parallel"),
        name="mixer_out_ln_shared_router",
    )(o, x, w_o.astype(MXU_DTYPE), ln_g.reshape(1, d), ln_b.reshape(1, d),
      sw1.astype(MXU_DTYPE), sw3.astype(MXU_DTYPE), sw2.astype(MXU_DTYPE), router_w.T.astype(MXU_DTYPE))


def _expert_kernel(blk_e_ref, xs_ref, w1_ref, w3_ref, w2_ref, y_ref):
    xe = xs_ref[...]
    hid = _silu(_dot(xe, w1_ref[0])) * _dot(xe, w3_ref[0])
    y_ref[...] = _dot(hid.astype(MXU_DTYPE), w2_ref[0]).astype(y_ref.dtype)


def _grouped_experts(xs, blk_e, w1, w3, w2):
    n_slots, d = xs.shape
    f = w1.shape[2]
    n_blocks = n_slots // EXPERT_BLOCK
    return pl.pallas_call(
        _expert_kernel,
        out_shape=jax.ShapeDtypeStruct((n_slots, d), MXU_DTYPE),
        grid_spec=pltpu.PrefetchScalarGridSpec(
            num_scalar_prefetch=1,
            grid=(n_blocks,),
            in_specs=[pl.BlockSpec((EXPERT_BLOCK, d), lambda i, be: (i, 0)),
                      pl.BlockSpec((1, d, f), lambda i, be: (be[i], 0, 0)),
                      pl.BlockSpec((1, d, f), lambda i, be: (be[i], 0, 0)),
                      pl.BlockSpec((1, f, d), lambda i, be: (be[i], 0, 0))],
            out_specs=pl.BlockSpec((EXPERT_BLOCK, d), lambda i, be: (i, 0)),
        ),
        compiler_params=_cparams("arbitrary"),
        name="routed_experts",
    )(blk_e, xs, w1.astype(MXU_DTYPE), w3.astype(MXU_DTYPE), w2.astype(MXU_DTYPE))


ROUTER_TM = 512
MOE_PARTS = 2


def _pick_rows(rows, row_id, n):
    out = jnp.zeros((n,) + rows[0].shape[1:], rows[0].dtype)
    for r in range(n):
        out = jnp.where(row_id == r, rows[r], out)
    return out


def _router_kernel(lg_ref, bias_ref, tri_ref, eidx_ref, gate_ref, rank_ref, cnt_ref, carry_ref):
    e, tm = lg_ref.shape
    per_group = e // N_GROUPS

    @pl.when(pl.program_id(0) == 0)
    def _():
        carry_ref[...] = jnp.zeros(carry_ref.shape, jnp.float32)

    s = _sigmoid(lg_ref[...])
    sb = s + bias_ref[...]
    neg_inf = -jnp.inf

    sub_id = lax.broadcasted_iota(jnp.int32, (per_group, tm), 0)
    g_rows = []
    for g in range(N_GROUPS):
        blk = sb[g * per_group:(g + 1) * per_group, :]
        m1 = jnp.max(blk, axis=0, keepdims=True)
        f1 = jnp.min(jnp.where(blk == m1, sub_id, per_group), axis=0, keepdims=True)
        m2 = jnp.max(jnp.where(sub_id == f1, neg_inf, blk), axis=0, keepdims=True)
        g_rows.append(m1 + m2)
    g_id = lax.broadcasted_iota(jnp.int32, (N_GROUPS, tm), 0)
    gscore = _pick_rows(g_rows, g_id, N_GROUPS)

    e_id = lax.broadcasted_iota(jnp.int32, (e, tm), 0)
    e_group = e_id // per_group if per_group & (per_group - 1) else e_id >> (per_group.bit_length() - 1)
    g_taken = jnp.zeros((N_GROUPS, tm), jnp.bool_)
    e_allowed = jnp.zeros((e, tm), jnp.bool_)
    for _ in range(TOPK_GROUPS):
        cand = jnp.where(g_taken, neg_inf, gscore)
        best = jnp.max(cand, axis=0, keepdims=True)
        first = jnp.min(jnp.where(cand == best, g_id, N_GROUPS), axis=0, keepdims=True)
        g_taken = g_taken | (g_id == first)
        e_allowed = e_allowed | (e_group == first)

    masked = jnp.where(e_allowed, sb, NEG)
    chosen = jnp.zeros((e, tm), jnp.bool_)
    id_rows, sel_rows = [], []
    for _ in range(TOP_K):
        cand = jnp.where(chosen, neg_inf, masked)
        best = jnp.max(cand, axis=0, keepdims=True)
        first = jnp.min(jnp.where(cand == best, e_id, e), axis=0, keepdims=True)
        hit = e_id == first
        chosen = chosen | hit
        id_rows.append(first)
        sel_rows.append(jnp.sum(jnp.where(hit, s, 0.0), axis=0, keepdims=True))
    k_id = lax.broadcasted_iota(jnp.int32, (TOP_K, tm), 0)
    eidx = _pick_rows(id_rows, k_id, TOP_K)
    sel = _pick_rows(sel_rows, k_id, TOP_K)
    eidx_ref[...] = eidx
    gate_ref[...] = sel / jnp.sum(sel, axis=0, keepdims=True) * ROUTED_SCALE

    chosen_f = jnp.where(chosen, 1.0, 0.0)
    incl = _dot(chosen_f.astype(MXU_DTYPE), tri_ref[...])
    rank_dense = carry_ref[...] + incl - chosen_f
    rank_rows = [jnp.sum(jnp.where(e_id == id_rows[r], rank_dense, 0.0), axis=0, keepdims=True)
                 for r in range(TOP_K)]
    rank_ref[...] = _pick_rows(rank_rows, k_id, TOP_K).astype(jnp.int32)
    carry_ref[...] = carry_ref[...] + jnp.sum(chosen_f, axis=1, keepdims=True)
    cnt_ref[...] = carry_ref[...]


def _route(logits_t, router_bias, part, n_parts):
    e, t_all = logits_t.shape
    t = t_all // n_parts
    tm = _pick_tm(t, ROUTER_TM)
    off = part * (t // tm)
    tri = jnp.asarray(np.triu(np.ones((tm, tm), np.float32)), dtype=MXU_DTYPE)
    kt = lambda dt: jax.ShapeDtypeStruct((TOP_K, t), dt)
    col = pl.BlockSpec((TOP_K, tm), lambda i: (0, i))
    eidx, gates, rank, cnt = pl.pallas_call(
        _router_kernel,
        out_shape=(kt(jnp.int32), kt(jnp.float32), kt(jnp.int32), jax.ShapeDtypeStruct((e, 1), jnp.float32)),
        grid=(t // tm,),
        in_specs=[pl.BlockSpec((e, tm), lambda i: (0, i + off)),
                  pl.BlockSpec((e, 1), lambda i: (0, 0)),
                  pl.BlockSpec((tm, tm), lambda i: (0, 0))],
        out_specs=(col, col, col, pl.BlockSpec((e, 1), lambda i: (0, 0))),
        scratch_shapes=[pltpu.VMEM((e, 1), jnp.float32)],
        compiler_params=_cparams("arbitrary"),
        name="moe_router",
    )(logits_t, router_bias.astype(jnp.float32).reshape(e, 1), tri)
    return eidx, gates, rank, cnt[:, 0].astype(jnp.int32)


def _moe_routed(x1b, eidx, rank, counts, w1, w3, w2, tok_offset):
    n_tok = eidx.shape[1]
    tk = n_tok * TOP_K
    n_blocks = (tk + N_EXPERTS * (EXPERT_BLOCK - 1)) // EXPERT_BLOCK + 1
    n_slots = n_blocks * EXPERT_BLOCK
    padded = (counts + EXPERT_BLOCK - 1) // EXPERT_BLOCK * EXPERT_BLOCK
    e_ids = jnp.arange(N_EXPERTS, dtype=jnp.int32)
    pad_end = jnp.sum(jnp.where(e_ids[None, :] <= e_ids[:, None], padded[None, :], 0), axis=1)
    start_pad = (pad_end - padded).astype(jnp.int32)
    start_of = jnp.sum(jnp.where(eidx[None] == e_ids[:, None, None], start_pad[:, None, None], 0), axis=0)
    dest = start_of + rank
    tok1 = jnp.broadcast_to(jnp.arange(n_tok, dtype=jnp.int32)[None, :] + 1, (TOP_K, n_tok))
    hit = jnp.zeros((n_slots,), jnp.int32).at[dest.reshape(tk)].add(tok1.reshape(tk), unique_indices=True)
    slot_tok = jnp.where(hit > 0, hit - 1, jnp.arange(n_slots, dtype=jnp.int32) % n_tok) + tok_offset
    blk_start = jnp.arange(n_blocks, dtype=jnp.int32) * EXPERT_BLOCK
    blk_e = jnp.minimum(jnp.sum((pad_end[None, :] <= blk_start[:, None]).astype(jnp.int32), axis=1),
                        N_EXPERTS - 1).astype(jnp.int32)
    y = _grouped_experts(x1b[slot_tok], blk_e, w1, w3, w2)
    return y, dest


def _post_kernel(x1_ref, sh_ref, yg_ref, gt_ref, p_ref, g_ref, b_ref, wp_ref, ng_ref, wg_ref, o_ref, *, alpha):
    gates = gt_ref[...]
    routed = yg_ref[0].astype(jnp.float32) * gates[:, 0:1]
    for k in range(1, TOP_K):
        routed = routed + yg_ref[k].astype(jnp.float32) * gates[:, k:k + 1]
    y = alpha * x1_ref[...] + (sh_ref[...] + routed)
    x2 = _layer_norm(y, g_ref[...], b_ref[...])
    e = _dot(p_ref[...].astype(MXU_DTYPE), wp_ref[...])
    e = e * lax.rsqrt(jnp.mean(e * e, axis=-1, keepdims=True) + RMS_EPS) * ng_ref[...]
    gate = _sigmoid(_dot(x2.astype(MXU_DTYPE), wg_ref[...]))
    o_ref[...] = x2 + gate * e


def _post_block(x1, shared, y_pairs, gates_t, p, ln_g, ln_b, w_proj, norm_g, w_gate, alpha, part, n_parts):
    t_all, d = x1.shape
    t = t_all // n_parts
    pd = p.shape[1]
    tm = _pick_tm(t, 256)
    off = part * (t // tm)
    row = lambda n: pl.BlockSpec((tm, n), lambda i: (i, 0))
    full_row = lambda n: pl.BlockSpec((tm, n), lambda i: (i + off, 0))
    res = lambda a, c: pl.BlockSpec((a, c), lambda i: (0, 0))
    return pl.pallas_call(
        functools.partial(_post_kernel, alpha=alpha),
        out_shape=jax.ShapeDtypeStruct((t, d), jnp.float32),
        grid=(t // tm,),
        in_specs=[full_row(d), full_row(d), pl.BlockSpec((TOP_K, tm, d), lambda i: (0, i, 0)), row(TOP_K),
                  full_row(pd), res(1, d), res(1, d), res(pd, d), res(1, d), res(d, d)],
        out_specs=row(d),
        compiler_params=_cparams("parallel"),
        name="moe_combine_ln_ple",
    )(x1, shared, y_pairs, gates_t, p, ln_g.reshape(1, d), ln_b.reshape(1, d), w_proj.astype(MXU_DTYPE),
      norm_g.reshape(1, d), w_gate.astype(MXU_DTYPE))


def _rope_tables(s_len):
    inv = 1.0 / (ROPE_THETA ** (np.arange(0, HEAD_DIM, 2, dtype=np.float32) / HEAD_DIM))
    ang = jnp.arange(s_len, dtype=jnp.float32)[:, None] * jnp.asarray(inv, dtype=jnp.float32)[None, :]
    return jnp.cos(ang), jnp.sin(ang)


def _rope(x, cos, sin):
    b, l, w = x.shape
    xh = x.reshape(b, l, w // HEAD_DIM, HEAD_DIM)
    half = HEAD_DIM // 2
    x1, x2 = xh[..., :half], xh[..., half:]
    c = cos[:, None, :]
    s = sin[:, None, :]
    return jnp.concatenate([x1 * c - x2 * s, x2 * c + x1 * s], axis=-1).reshape(b, l, w)


def _split_cols(h, sizes):
    out, c = [], 0
    for n in sizes:
        out.append(h[..., c:c + n])
        c += n
    return out


def _even_mixer(x, w_in, b_gate, pos_k, w1_k, w2_k, pos_v, w1_v, w2_v, cos, sin):
    b, s, d = x.shape
    mix = A_HEADS * HEAD_DIM
    q_a, q_i, q_b, rest = _even_projection(x.reshape(b * s, d), w_in, s)
    q_a, q_i, q_b = (q.reshape(b, s, mix) for q in (q_a, q_i, q_b))
    k_a, v_a, k_i, w_i, kv_b, g_b = _split_cols(rest.reshape(b, s, -1),
                                                (HEAD_DIM, HEAD_DIM, IDX_DIM, IDX_HEADS, 6 * HEAD_DIM, 3 * B_HEADS))
    o_a = _dsa_attention(q_a, _rope(k_a, cos, sin), v_a, q_i, _rope(k_i, cos, sin), w_i)
    k_c, v_c, k_s, v_s, k_w, v_w = _split_cols(kv_b, (HEAD_DIM,) * 6)
    nc = s // CMP_STRIDE
    c_last = jnp.minimum(jnp.arange(nc) * CMP_STRIDE + CMP_LEN - 1, s - 1)
    kc = _rope(_compress(k_c, pos_k, w1_k, w2_k), cos[c_last], sin[c_last])
    vc = _compress(v_c, pos_v, w1_v, w2_v)
    gate = (g_b + b_gate).reshape(b, s, B_HEADS, 3)
    o_b = _nsa_attention(q_b, kc, vc, _rope(k_s, cos, sin), v_s, _rope(k_w, cos, sin), v_w, gate)
    return jnp.concatenate([o_a, o_b], axis=-1).reshape(b * s, 2 * mix)


def _odd_mixer(x, w_in, b_f):
    b, s, d = x.shape
    mix = C_HEADS * HEAD_DIM
    h = _matmul(x.reshape(b * s, d), w_in, name="odd_in_proj").reshape(b, s, -1)
    f = h[..., 3 * mix:3 * mix + C_HEADS]
    log_f = jax.nn.log_sigmoid((f + b_f).astype(jnp.float32))
    return _fox_attention(h, log_f).reshape(b * s, mix)


def kernel(x, p, ev_w_in, ev_b_gate, ev_cmp_pos_k, ev_cmp_w1_k, ev_cmp_w2_k, ev_cmp_pos_v, ev_cmp_w1_v, ev_cmp_w2_v, ev_w_o, od_w_in, od_b_f, od_w_o, ln1_g, ln1_b, ln2_g, ln2_b, router_w, router_bias, exp_w1, exp_w3, exp_w2, sh_w1, sh_w3, sh_w2, ple_w_gate, ple_w_proj, ple_norm_g):
    b, s, d = x.shape
    depth = p.shape[0]
    alpha = float((2.0 * depth) ** 0.25)
    cos, sin = _rope_tables(s)
    xf = x.reshape(b * s, d)
    for i in range(depth):
        j = i // 2
        x3 = xf.reshape(b, s, d)
        if i % 2 == 0:
            o = _even_mixer(x3, ev_w_in[j], ev_b_gate[j], ev_cmp_pos_k[j], ev_cmp_w1_k[j], ev_cmp_w2_k[j],
                            ev_cmp_pos_v[j], ev_cmp_w1_v[j], ev_cmp_w2_v[j], cos, sin)
            w_o = ev_w_o[j]
        else:
            o = _odd_mixer(x3, od_w_in[j], od_b_f[j])
            w_o = od_w_o[j]
        x1, x1b, shared, logits_t = _mid_block(o, xf, w_o, ln1_g[i], ln1_b[i], sh_w1[i], sh_w3[i], sh_w2[i],
                                               router_w[i], alpha)
        done = []
        for part in range(MOE_PARTS):
            eidx, gates, rank, counts = _route(logits_t, router_bias[i], part, MOE_PARTS)
            y, dest = _moe_routed(x1b, eidx, rank, counts, exp_w1[i], exp_w3[i], exp_w2[i],
                                  part * (b * s // MOE_PARTS))
            done.append(_post_block(x1, shared, y[dest], gates.T, p[i].reshape(b * s, -1), ln2_g[i], ln2_b[i],
                                    ple_w_proj[i], ple_norm_g[i], ple_w_gate[i], alpha, part, MOE_PARTS))
        xf = jnp.concatenate(done, axis=0)
    return xf.reshape(b, s, d)
```

```python
import functools

import numpy as np
import jax
import jax.numpy as jnp
from jax import lax
from jax.experimental import pallas as pl
from jax.experimental.pallas import tpu as pltpu

HEAD_DIM = 64
QBLOCK = 128
ROPE_THETA = 10000.0
LN_EPS = 1e-5
RMS_EPS = 1e-6
NEG = -1e30
BIG = 1e30

A_HEADS = 8
IDX_HEADS = 8
IDX_DIM = 64
DSA_TOPK_MAX = 256

B_HEADS = 8
CMP_LEN = 32
CMP_STRIDE = 16
SEL_LEN = 32
SEL_BLOCKS_MAX = 8
WINDOW = 512

C_HEADS = 16

N_EXPERTS = 64
TOP_K = 8
N_GROUPS = 8
TOPK_GROUPS = 4
ROUTED_SCALE = 2.5
EXPERT_BLOCK = 512

LANES = 128
SUBLANES = 8
VMEM_LIMIT = 48 * 1024 * 1024

MXU_DTYPE = jnp.bfloat16
KEY_TILE = 256
ATTN_SCALE = float(HEAD_DIM ** -0.5 * np.log2(np.e))
LOG2E = float(np.log2(np.e))
INT_MIN = -(2 ** 31)
IDX_ALL = 2 ** 30


def _order_key(bits):
    return bits ^ ((bits >> 31) & 0x7FFFFFFF)


NEG_KEY = int(_order_key(np.float32(NEG).view(np.int32).astype(np.int64)).astype(np.int32))


def _cparams(*sem):
    return pltpu.CompilerParams(dimension_semantics=sem, vmem_limit_bytes=VMEM_LIMIT)


def _dot(a, b):
    return jnp.dot(a, b, preferred_element_type=jnp.float32)


def _dot_nt(a, b):
    return lax.dot_general(a, b, (((1,), (1,)), ((), ())), preferred_element_type=jnp.float32)


def _silu(x):
    return x * (1.0 / (1.0 + jnp.exp(-x)))


def _sigmoid(x):
    return 1.0 / (1.0 + jnp.exp(-x))


def _mm_kernel(x_ref, w_ref, o_ref):
    o_ref[...] = _dot(x_ref[...].astype(MXU_DTYPE), w_ref[...])


def _pick_tm(t, cap):
    tm = min(cap, t)
    while t % tm:
        tm //= 2
    return tm


def _matmul(x, w, *, tm_cap=512, name="proj"):
    t, k = x.shape
    n = w.shape[1]
    n_pad = -(-n // LANES) * LANES
    w = jnp.pad(w, ((0, 0), (0, n_pad - n))).astype(MXU_DTYPE)
    tm = _pick_tm(t, tm_cap)
    out = pl.pallas_call(
        _mm_kernel,
        out_shape=jax.ShapeDtypeStruct((t, n_pad), jnp.float32),
        grid=(t // tm,),
        in_specs=[pl.BlockSpec((tm, k), lambda i: (i, 0)),
                  pl.BlockSpec((k, n_pad), lambda i: (0, 0))],
        out_specs=pl.BlockSpec((tm, n_pad), lambda i: (i, 0)),
        compiler_params=_cparams("parallel"),
        name=name,
    )(x, w)
    return out


def _rope_lanes(x, cos, sin_signed):
    lane = lax.broadcasted_iota(jnp.int32, (x.shape[0], LANES), 1)
    first_half = (lane % HEAD_DIM) < HEAD_DIM // 2
    out = []
    for c in range(x.shape[1] // LANES):
        blk = x[:, c * LANES:(c + 1) * LANES]
        partner = jnp.where(first_half, pltpu.roll(blk, LANES - HEAD_DIM // 2, 1), pltpu.roll(blk, HEAD_DIM // 2, 1))
        out.append(blk * cos + partner * sin_signed)
    return jnp.concatenate(out, axis=1)


def _even_proj_kernel(x_ref, wq_ref, wr_ref, cos_ref, sin_ref, qa_ref, qi_ref, qb_ref, rest_ref):
    xb = x_ref[...].astype(MXU_DTYPE)
    cos = cos_ref[...]
    sin = sin_ref[...]
    for g, (o_ref, scale) in enumerate(((qa_ref, ATTN_SCALE), (qi_ref, 1.0), (qb_ref, ATTN_SCALE))):
        roped = _rope_lanes(_dot(xb, wq_ref[g]), cos, sin)
        o_ref[...] = (roped * scale).astype(o_ref.dtype)
    rest_ref[...] = _dot(xb, wr_ref[...])


def _even_projection(x, w_in, s_len):
    t, d = x.shape
    mix = A_HEADS * HEAD_DIM
    q_cols = [(0, mix), (mix + 2 * HEAD_DIM, IDX_HEADS * IDX_DIM),
              (mix + 2 * HEAD_DIM + IDX_HEADS * IDX_DIM + IDX_DIM + IDX_HEADS, B_HEADS * HEAD_DIM)]
    wq = jnp.stack([w_in[:, c:c + n] for c, n in q_cols]).astype(MXU_DTYPE)
    keep = np.ones(w_in.shape[1], bool)
    for c, n in q_cols:
        keep[c:c + n] = False
    rest_cols = np.nonzero(keep)[0]
    n_rest = -(-len(rest_cols) // LANES) * LANES
    wr = jnp.pad(w_in[:, rest_cols], ((0, 0), (0, n_rest - len(rest_cols)))).astype(MXU_DTYPE)
    inv = 1.0 / (ROPE_THETA ** (np.arange(0, HEAD_DIM, 2, dtype=np.float32) / HEAD_DIM))
    lane = np.arange(LANES)
    ang = jnp.arange(s_len, dtype=jnp.float32)[:, None] * jnp.asarray(inv[lane % (HEAD_DIM // 2)])[None, :]
    sign = np.where((lane % HEAD_DIM) < HEAD_DIM // 2, -1.0, 1.0).astype(np.float32)
    cos_t, sin_t = jnp.cos(ang), jnp.sin(ang) * sign[None, :]
    tm = _pick_tm(s_len, 512)
    per_s = s_len // tm
    row = lambda n: pl.BlockSpec((tm, n), lambda i: (i, 0))
    tab = pl.BlockSpec((tm, LANES), lambda i: (i % per_s, 0))
    qshape = jax.ShapeDtypeStruct((t, mix), MXU_DTYPE)
    qa, qi, qb, rest = pl.pallas_call(
        _even_proj_kernel,
        out_shape=(qshape, qshape, qshape, jax.ShapeDtypeStruct((t, n_rest), jnp.float32)),
        grid=(t // tm,),
        in_specs=[row(d), pl.BlockSpec((3, d, mix), lambda i: (0, 0, 0)), pl.BlockSpec((d, n_rest), lambda i: (0, 0)),
                  tab, tab],
        out_specs=(row(mix), row(mix), row(mix), row(n_rest)),
        compiler_params=_cparams("parallel"),
        name="even_in_proj",
    )(x, wq, wr, cos_t, sin_t)
    return qa, qi, qb, rest[:, :len(rest_cols)]


def _online_update(s, drop, vt, m_ref, l_ref, acc_ref, idx=Ellipsis):
    if drop is not None:
        s = s + drop
    m_old = m_ref[idx]
    m_new = jnp.maximum(m_old, jnp.max(s, axis=0, keepdims=True))
    p = jnp.exp2(s - m_new)
    alpha = jnp.exp2(m_old - m_new)
    l_ref[idx] = alpha * l_ref[idx] + jnp.sum(p, axis=0, keepdims=True)
    acc_ref[idx] = alpha * acc_ref[idx] + _dot(vt, p.astype(vt.dtype))
    m_ref[idx] = m_new


def _drop(pred, reps=1):
    d = jnp.where(pred, 0.0, NEG)
    return d if reps == 1 else jnp.concatenate([d] * reps, axis=1)


def _init_state(m_ref, l_ref, acc_ref):
    m_ref[...] = jnp.full(m_ref.shape, NEG, jnp.float32)
    l_ref[...] = jnp.zeros(l_ref.shape, jnp.float32)
    acc_ref[...] = jnp.zeros(acc_ref.shape, jnp.float32)


def _normalized(l_ref, acc_ref, idx=Ellipsis):
    return acc_ref[idx] / l_ref[idx]


def _pipelined_tiles(lo, hi, produce, consume, consume_last=None):
    consume_last = consume_last or consume
    produce(lo, 0)

    def pair(j, carry):
        kt = lo + 2 * j
        produce(kt + 1, 1)
        consume(kt, 0)
        produce(kt + 2, 0)
        consume(kt + 1, 1)
        return carry

    n_body = hi - lo - 1
    lax.fori_loop(0, n_body // 2, pair, 0)

    @pl.when(n_body % 2 == 0)
    def _():
        consume_last(hi - 1, 0)

    @pl.when(n_body % 2 == 1)
    def _():
        produce(hi - 1, 1)
        consume(hi - 2, 0)
        consume_last(hi - 1, 1)


def _stack_heads(q, n_heads):
    return jnp.concatenate([q[:, h * HEAD_DIM:(h + 1) * HEAD_DIM] for h in range(n_heads)], axis=0)


def _store_heads(o_t, o_ref, n_heads, tq):
    for j in range(n_heads // 2):
        pair = jnp.concatenate([o_t[:, (2 * j) * tq:(2 * j + 1) * tq],
                                o_t[:, (2 * j + 1) * tq:(2 * j + 2) * tq]], axis=0)
        o_ref[0, :, 2 * j * HEAD_DIM:(2 * j + 2) * HEAD_DIM] = pair.T.astype(o_ref.dtype)


def _dsa_kernel(qi_ref, w_ref, ki_ref, qa_ref, ka_ref, vat_ref, o_ref,
                key_ref, m_ref, l_ref, acc_ref, sbuf_ref, *, k_sel, idx_scale):
    i = pl.program_id(1)
    tq = QBLOCK
    tk = KEY_TILE
    n_tiles = (i * tq + tq + tk - 1) // tk
    q0 = i * tq
    t_row = q0 + lax.broadcasted_iota(jnp.int32, (1, tq), 1)
    kk = lax.broadcasted_iota(jnp.int32, (tk, tq), 0)

    qi = _stack_heads(qi_ref[0], IDX_HEADS)
    w_row = w_ref[0, 0]

    def tile_start(kt):
        return pl.multiple_of(kt * tk, tk)

    def logits_tile(kt, slot):
        sbuf_ref[slot] = _dot_nt(ki_ref[0, pl.ds(tile_start(kt), tk), :], qi)

    def score_tile(kt, slot):
        k0 = tile_start(kt)
        z = jnp.maximum(sbuf_ref[slot], 0.0) * w_row
        sc = z[:, 0:tq]
        for h in range(1, IDX_HEADS):
            sc = sc + z[:, h * tq:(h + 1) * tq]
        sc = sc * idx_scale
        sc = jnp.where(k0 + kk <= t_row, sc, NEG)
        bits = lax.bitcast_convert_type(sc, jnp.int32)
        key_ref[pl.ds(k0, tk), :] = _order_key(bits)

    _pipelined_tiles(0, n_tiles, logits_tile, score_tile)

    @pl.when(n_tiles % 2 == 1)
    def _():
        key_ref[pl.ds(tile_start(n_tiles), tk), :] = jnp.full((tk, tq), NEG_KEY, jnp.int32)

    ct = 2 * tk
    kk2 = lax.broadcasted_iota(jnp.int32, (ct, tq), 0)

    def count(pred_fn):
        def body(kt, acc):
            k0 = pl.multiple_of(kt * ct, ct)
            c = pred_fn(key_ref[pl.ds(k0, ct), :], k0 + kk2).astype(jnp.int32)
            return acc + jnp.sum(c.reshape(ct // SUBLANES, SUBLANES, tq), axis=0)
        acc = lax.fori_loop(0, (n_tiles + 1) // 2, body, jnp.zeros((SUBLANES, tq), jnp.int32))
        return jnp.sum(acc, axis=0, keepdims=True)

    def search(_):
        c0 = count(lambda key, idx: key >= 0)
        thr = jnp.where(c0 >= k_sel, 0, INT_MIN).astype(jnp.int32)

        def bit_step(b, thr):
            trial = thr | (jnp.int32(1) << (30 - b))
            c = count(lambda key, idx: key >= trial)
            return jnp.where(c >= k_sel, trial, thr)

        thr = lax.fori_loop(0, 31, bit_step, thr)
        c_ge = count(lambda key, idx: key >= thr)

        def tie_search(_):
            c_gt = count(lambda key, idx: key > thr)
            need = k_sel - c_gt

            def idx_step(b, u):
                trial = u | (jnp.int32(1) << (20 - b))
                c = count(lambda key, idx: (key == thr) & (idx < trial))
                return jnp.where(c < need, trial, u)

            return lax.fori_loop(0, 21, idx_step, jnp.zeros((1, tq), jnp.int32))

        cut = lax.cond(jnp.max(c_ge) > k_sel, tie_search,
                       lambda _: jnp.full((1, tq), IDX_ALL, jnp.int32), 0)
        return thr, cut

    thr, cut = lax.cond(q0 + tq > k_sel, search,
                        lambda _: (jnp.full((1, tq), INT_MIN, jnp.int32),
                                   jnp.full((1, tq), IDX_ALL, jnp.int32)), 0)

    qa = _stack_heads(qa_ref[0], A_HEADS)
    _init_state(m_ref, l_ref, acc_ref)

    def qk_tile(kt, slot):
        sbuf_ref[slot] = _dot_nt(ka_ref[0, pl.ds(tile_start(kt), tk), :], qa)

    def attn_tile(kt, slot):
        k0 = tile_start(kt)
        key = key_ref[pl.ds(k0, tk), :]
        idx = k0 + kk
        sel = ((key > thr) | ((key == thr) & (idx <= cut))) & (idx <= t_row)
        _online_update(sbuf_ref[slot], _drop(sel, A_HEADS), vat_ref[0, kt], m_ref, l_ref, acc_ref)

    _pipelined_tiles(0, n_tiles, qk_tile, attn_tile)
    _store_heads(_normalized(l_ref, acc_ref), o_ref, A_HEADS, tq)


def _value_tiles_t(v, tk):
    b, s, d = v.shape
    return jnp.transpose(v.reshape(b, s // tk, tk, d), (0, 1, 3, 2)).astype(MXU_DTYPE)


def _per_query_rows(x, tq):
    b, s, h = x.shape
    return jnp.transpose(x.reshape(b, s // tq, tq, h), (0, 1, 3, 2)).reshape(b, s // tq, 1, h * tq)


def _dsa_attention(q_a, k_a, v_a, q_i, k_i, w_i):
    b, s, _ = q_a.shape
    tq = QBLOCK
    nq = s // tq
    k_sel = min(DSA_TOPK_MAX, s // 4)
    tk = KEY_TILE
    assert (s // tk) % 2 == 0
    kern = functools.partial(_dsa_kernel, k_sel=k_sel, idx_scale=float((IDX_HEADS * IDX_DIM) ** -0.5))
    hq = A_HEADS * tq
    return pl.pallas_call(
        kern,
        out_shape=jax.ShapeDtypeStruct((b, s, A_HEADS * HEAD_DIM), MXU_DTYPE),
        grid=(b, nq),
        in_specs=[
            pl.BlockSpec((1, tq, IDX_HEADS * IDX_DIM), lambda bi, i: (bi, i, 0)),
            pl.BlockSpec((1, 1, 1, hq), lambda bi, i: (bi, i, 0, 0)),
            pl.BlockSpec((1, s, IDX_DIM), lambda bi, i: (bi, 0, 0)),
            pl.BlockSpec((1, tq, A_HEADS * HEAD_DIM), lambda bi, i: (bi, i, 0)),
            pl.BlockSpec((1, s, HEAD_DIM), lambda bi, i: (bi, 0, 0)),
            pl.BlockSpec((1, s // tk, HEAD_DIM, tk), lambda bi, i: (bi, 0, 0, 0)),
        ],
        out_specs=pl.BlockSpec((1, tq, A_HEADS * HEAD_DIM), lambda bi, i: (bi, i, 0)),
        scratch_shapes=[pltpu.VMEM((s, tq), jnp.int32),
                        pltpu.VMEM((1, hq), jnp.float32),
                        pltpu.VMEM((1, hq), jnp.float32),
                        pltpu.VMEM((HEAD_DIM, hq), jnp.float32),
                        pltpu.VMEM((2, tk, hq), jnp.float32)],
        compiler_params=_cparams("parallel", "arbitrary"),
        name="dsa_attention",
    )(q_i, _per_query_rows(w_i.astype(jnp.float32), tq), k_i.astype(MXU_DTYPE), q_a, k_a.astype(MXU_DTYPE),
      _value_tiles_t(v_a, tk))


def _cmp_kernel(ck_ref, pe_ref, w1_ref, w2_ref, o_ref):
    nc = ck_ref.shape[1]
    half = ck_ref.shape[2]
    ck = ck_ref[0]
    a = _dot((ck + pe_ref[0:1, :]).astype(MXU_DTYPE), w1_ref[0:half, :])
    bm = _dot((ck + pe_ref[1:2, :]).astype(MXU_DTYPE), w1_ref[half:2 * half, :])
    h = a + pltpu.roll(bm, nc - 1, 0)
    o_ref[0] = _dot(_silu(h).astype(MXU_DTYPE), w2_ref[...])


def _compress(kv, pe, w1, w2):
    b, s, d = kv.shape
    nc = s // CMP_STRIDE
    half = CMP_STRIDE * d
    ck = kv.reshape(b, nc, half)
    pe2 = pe.reshape(2, half)
    return pl.pallas_call(
        _cmp_kernel,
        out_shape=jax.ShapeDtypeStruct((b, nc, d), jnp.float32),
        grid=(b,),
        in_specs=[pl.BlockSpec((1, nc, half), lambda bi: (bi, 0, 0)),
                  pl.BlockSpec((2, half), lambda bi: (0, 0)),
                  pl.BlockSpec((2 * half, d), lambda bi: (0, 0)),
                  pl.BlockSpec((d, d), lambda bi: (0, 0))],
        out_specs=pl.BlockSpec((1, nc, d), lambda bi: (bi, 0, 0)),
        compiler_params=_cparams("parallel"),
        name="nsa_compress",
    )(ck, pe2, w1.astype(MXU_DTYPE), w2.astype(MXU_DTYPE))


def _split3(x):
    def top(v):
        return lax.bitcast_convert_type(lax.bitcast_convert_type(v, jnp.int32) & jnp.int32(-65536), jnp.float32)
    hi = top(x)
    r1 = x - hi
    mid = top(r1)
    lo = r1 - mid
    return hi.astype(MXU_DTYPE), mid.astype(MXU_DTYPE), lo.astype(MXU_DTYPE)


def _nsa_kernel(q_ref, g_ref, kc_ref, vct_ref, ovt_ref, exp_ref, ks_ref, vst_ref, kw_ref, vwt_ref, o_ref,
                tok_ref, m_ref, l_ref, acc_ref, m2_ref, l2_ref, acc2_ref, out_ref, sbuf_ref, *, n_sel, n_cmp):
    i = pl.program_id(1)
    tq = QBLOCK
    tk = KEY_TILE
    nh = B_HEADS
    hq = nh * tq
    q0 = i * tq
    n_tiles = (q0 + tq + tk - 1) // tk
    t_row = q0 + lax.broadcasted_iota(jnp.int32, (1, tq), 1)
    kk = lax.broadcasted_iota(jnp.int32, (tk, tq), 0)
    q = _stack_heads(q_ref[0], nh)
    gates = g_ref[0, 0]

    nc = kc_ref.shape[1]
    c_id = lax.broadcasted_iota(jnp.int32, (nc, tq), 0)
    valid_c = (c_id * CMP_STRIDE + (CMP_LEN - 1) <= t_row) & (c_id < n_cmp)
    keep_c = jnp.concatenate([jnp.where(valid_c, 1.0, 0.0)] * nh, axis=1)
    s_c = _dot_nt(kc_ref[0], q) + _drop(valid_c, nh)
    e_c = jnp.exp2(s_c - jnp.max(s_c, axis=0, keepdims=True)) * keep_c
    den = jnp.sum(e_c, axis=0, keepdims=True)
    p_c = e_c / jnp.where(den > 0.0, den, 1.0)
    out_ref[...] = gates[0:1, :] * _dot(vct_ref[0], p_c.astype(MXU_DTYPE))

    p_sum = p_c[:, 0:tq]
    for h in range(1, nh):
        p_sum = p_sum + p_c[:, h * tq:(h + 1) * tq]
    ovt = ovt_ref[...]
    pieces = _split3(p_sum)
    imp = _dot(ovt, pieces[0]) + _dot(ovt, pieces[1]) + _dot(ovt, pieces[2])
    n_sb = ovt.shape[0]
    j_id = lax.broadcasted_iota(jnp.int32, (n_sb, tq), 0)
    cur = t_row >> (SEL_LEN.bit_length() - 1)
    forced = (j_id == 0) | (j_id == cur) | (j_id == cur - 1)
    future = j_id * SEL_LEN > t_row
    imp = jnp.where(forced, BIG, jnp.where(future, NEG, imp))
    chosen = jnp.zeros((n_sb, tq), jnp.bool_)
    for _ in range(n_sel):
        cand = jnp.where(chosen, -jnp.inf, imp)
        best = jnp.max(cand, axis=0, keepdims=True)
        first = jnp.min(jnp.where(cand == best, j_id, n_sb), axis=0, keepdims=True)
        chosen = chosen | (j_id == first)
    tok_ref[...] = _dot(exp_ref[...], jnp.where(chosen, 1.0, 0.0).astype(MXU_DTYPE))

    _init_state(m_ref, l_ref, acc_ref)
    _init_state(m2_ref, l2_ref, acc2_ref)
    win_lo = jnp.maximum(q0 - WINDOW, 0) // tk

    def tile_start(kt):
        return pl.multiple_of(kt * tk, tk)

    def sel_qk(kt, slot):
        sbuf_ref[slot] = _dot_nt(ks_ref[0, pl.ds(tile_start(kt), tk), :], q)

    def sel_tile(kt, slot):
        k0 = tile_start(kt)
        sel = (tok_ref[pl.ds(k0, tk), :] > 0.5) & (k0 + kk <= t_row)
        _online_update(sbuf_ref[slot], _drop(sel, nh), vst_ref[0, kt], m_ref, l_ref, acc_ref)

    def win_qk(kt, slot):
        sbuf_ref[slot] = _dot_nt(kw_ref[0, pl.ds(tile_start(kt), tk), :], q)

    def win_tile(kt, slot):
        kpos = tile_start(kt) + kk
        ok = (kpos <= t_row) & (kpos > t_row - WINDOW)
        _online_update(sbuf_ref[slot], _drop(ok, nh), vwt_ref[0, kt], m2_ref, l2_ref, acc2_ref)

    _pipelined_tiles(0, n_tiles, sel_qk, sel_tile)
    _pipelined_tiles(win_lo, n_tiles, win_qk, win_tile)
    o_t = out_ref[...] + gates[1:2, :] * _normalized(l_ref, acc_ref) + gates[2:3, :] * _normalized(l2_ref, acc2_ref)
    _store_heads(o_t, o_ref, nh, tq)


def _nsa_attention(q_b, k_c, v_c, k_s, v_s, k_w, v_w, gate):
    b, s, _ = q_b.shape
    tq = QBLOCK
    nq = s // tq
    nh = B_HEADS
    hq = nh * tq
    nc = k_c.shape[1]
    n_cmp = (s - CMP_LEN) // CMP_STRIDE + 1
    n_sb = s // SEL_LEN
    n_sel = min(SEL_BLOCKS_MAX, n_sb)
    c_start = np.arange(nc) * CMP_STRIDE
    s_start = np.arange(n_sb) * SEL_LEN
    ov = np.clip(np.minimum(c_start[:, None] + CMP_LEN, s_start[None, :] + SEL_LEN)
                 - np.maximum(c_start[:, None], s_start[None, :]), 0, None) / CMP_LEN
    ov[n_cmp:] = 0.0
    ovt = jnp.asarray(ov.T, dtype=MXU_DTYPE)
    expand = jnp.asarray((np.arange(s)[:, None] // SEL_LEN == np.arange(n_sb)[None, :]), dtype=MXU_DTYPE)
    g = _sigmoid(gate.astype(jnp.float32))
    g = jnp.transpose(g.reshape(b, nq, tq, nh, 3), (0, 1, 4, 3, 2)).reshape(b, nq, 3, hq)
    tk = KEY_TILE
    kern = functools.partial(_nsa_kernel, n_sel=n_sel, n_cmp=n_cmp)
    full = lambda shape: pl.BlockSpec(shape, lambda bi, i: (0,) * len(shape))
    per_b = lambda shape: pl.BlockSpec(shape, lambda bi, i: (bi,) + (0,) * (len(shape) - 1))
    state = [pltpu.VMEM((1, hq), jnp.float32), pltpu.VMEM((1, hq), jnp.float32),
             pltpu.VMEM((HEAD_DIM, hq), jnp.float32)]
    return pl.pallas_call(
        kern,
        out_shape=jax.ShapeDtypeStruct((b, s, nh * HEAD_DIM), MXU_DTYPE),
        grid=(b, nq),
        in_specs=[
            pl.BlockSpec((1, tq, nh * HEAD_DIM), lambda bi, i: (bi, i, 0)),
            pl.BlockSpec((1, 1, 3, hq), lambda bi, i: (bi, i, 0, 0)),
            per_b((1, nc, HEAD_DIM)),
            per_b((1, HEAD_DIM, nc)),
            full((n_sb, nc)),
            full((s, n_sb)),
            per_b((1, s, HEAD_DIM)),
            per_b((1, s // tk, HEAD_DIM, tk)),
            per_b((1, s, HEAD_DIM)),
            per_b((1, s // tk, HEAD_DIM, tk)),
        ],
        out_specs=pl.BlockSpec((1, tq, nh * HEAD_DIM), lambda bi, i: (bi, i, 0)),
        scratch_shapes=[pltpu.VMEM((s, tq), jnp.float32)] + state + state
                       + [pltpu.VMEM((HEAD_DIM, hq), jnp.float32), pltpu.VMEM((2, tk, hq), jnp.float32)],
        compiler_params=_cparams("parallel", "arbitrary"),
        name="nsa_attention",
    )(q_b, g, k_c.astype(MXU_DTYPE),
      jnp.transpose(v_c, (0, 2, 1)).astype(MXU_DTYPE), ovt, expand,
      k_s.astype(MXU_DTYPE), _value_tiles_t(v_s, tk), k_w.astype(MXU_DTYPE), _value_tiles_t(v_w, tk))


FOX_TQ = 256
FOX_HB = 8
FOX_KPAD = 128


def _fox_kernel(q_ref, k_ref, v_ref, d3_ref, o_ref, m_ref, l_ref, acc_ref, sbuf_ref, qaug_ref, kaug_ref, vt_ref):
    i = pl.program_id(2)
    tq = FOX_TQ
    s_len = k_ref.shape[1]

    @pl.when(i == 0)
    def _():
        for hh in range(FOX_HB):
            k_h = k_ref[0, :, hh * HEAD_DIM:(hh + 1) * HEAD_DIM]
            kaug_ref[hh] = jnp.concatenate([k_h, d3_ref[0, hh].astype(jnp.float32)], axis=1).astype(kaug_ref.dtype)
        for c in range(s_len // tq):
            for pr in range(FOX_HB // 2):
                blk_t = v_ref[0, c * tq:(c + 1) * tq, 2 * pr * HEAD_DIM:(2 * pr + 2) * HEAD_DIM].T
                vt_ref[2 * pr, c] = blk_t[0:HEAD_DIM].astype(vt_ref.dtype)
                vt_ref[2 * pr + 1, c] = blk_t[HEAD_DIM:2 * HEAD_DIM].astype(vt_ref.dtype)

    lane = lax.broadcasted_iota(jnp.int32, (tq, HEAD_DIM), 1)
    ones_cols = jnp.where(lane < 3, 1.0, 0.0)
    for hh in range(FOX_HB):
        q_h = q_ref[0, :, hh * HEAD_DIM:(hh + 1) * HEAD_DIM] * ATTN_SCALE
        qaug_ref[hh] = jnp.concatenate([q_h, ones_cols], axis=1).astype(qaug_ref.dtype)
    _init_state(m_ref, l_ref, acc_ref)

    def qk_tile(kt, slot):
        k0 = pl.multiple_of(kt * tq, tq)
        for hh in range(FOX_HB):
            sbuf_ref[slot, hh] = _dot_nt(kaug_ref[hh, pl.ds(k0, tq), :], qaug_ref[hh])

    def update(kt, slot, drop):
        for hh in range(FOX_HB):
            _online_update(sbuf_ref[slot, hh], drop, vt_ref[hh, kt], m_ref, l_ref, acc_ref, idx=hh)

    def diag_tile(kt, slot):
        kk = lax.broadcasted_iota(jnp.int32, (tq, tq), 0)
        qq = lax.broadcasted_iota(jnp.int32, (tq, tq), 1)
        update(kt, slot, _drop(kk <= qq))

    _pipelined_tiles(0, i + 1, qk_tile, lambda kt, slot: update(kt, slot, None), diag_tile)
    for j in range(FOX_HB // 2):
        pair = jnp.concatenate([_normalized(l_ref, acc_ref, 2 * j), _normalized(l_ref, acc_ref, 2 * j + 1)], axis=0)
        o_ref[0, :, 2 * j * HEAD_DIM:(2 * j + 2) * HEAD_DIM] = pair.T.astype(o_ref.dtype)


def _fox_attention(h, log_f):
    b, s, _ = h.shape
    nh = C_HEADS
    tq = FOX_TQ
    assert s % tq == 0
    gw = FOX_HB * HEAD_DIM
    ng = nh // FOX_HB
    d_cum = jnp.cumsum(log_f, axis=1)
    d3 = jnp.stack(_split3(d_cum * LOG2E), axis=-1)
    d3 = jnp.pad(jnp.transpose(-d3, (0, 2, 1, 3)), ((0, 0), (0, 0), (0, 0), (0, HEAD_DIM - 3)))
    return pl.pallas_call(
        _fox_kernel,
        out_shape=jax.ShapeDtypeStruct((b, s, nh * HEAD_DIM), MXU_DTYPE),
        grid=(b, ng, s // tq),
        in_specs=[
            pl.BlockSpec((1, tq, gw), lambda bi, j, i: (bi, i, j)),
            pl.BlockSpec((1, s, gw), lambda bi, j, i: (bi, 0, ng + j)),
            pl.BlockSpec((1, s, gw), lambda bi, j, i: (bi, 0, 2 * ng + j)),
            pl.BlockSpec((1, FOX_HB, s, HEAD_DIM), lambda bi, j, i: (bi, j, 0, 0)),
        ],
        out_specs=pl.BlockSpec((1, tq, gw), lambda bi, j, i: (bi, i, j)),
        scratch_shapes=[pltpu.VMEM((FOX_HB, 1, tq), jnp.float32),
                        pltpu.VMEM((FOX_HB, 1, tq), jnp.float32),
                        pltpu.VMEM((FOX_HB, HEAD_DIM, tq), jnp.float32),
                        pltpu.VMEM((2, FOX_HB, tq, tq), jnp.float32),
                        pltpu.VMEM((FOX_HB, tq, FOX_KPAD), MXU_DTYPE),
                        pltpu.VMEM((FOX_HB, s, FOX_KPAD), MXU_DTYPE),
                        pltpu.VMEM((FOX_HB, s // tq, HEAD_DIM, tq), MXU_DTYPE)],
        compiler_params=_cparams("parallel", "arbitrary", "arbitrary"),
        name="fox_attention",
    )(h, h, h, d3)


def _layer_norm(y, g, b):
    mu = jnp.mean(y, axis=-1, keepdims=True)
    d = y - mu
    var = jnp.mean(d * d, axis=-1, keepdims=True)
    return d * lax.rsqrt(var + LN_EPS) * g + b


def _mid_kernel(o_ref, x_ref, wo_ref, g_ref, b_ref, sw1_ref, sw3_ref, sw2_ref, rw_ref,
                x1_ref, x1b_ref, sh_ref, lg_ref, *, alpha):
    y = alpha * x_ref[...] + _dot(o_ref[...], wo_ref[...])
    x1 = _layer_norm(y, g_ref[...], b_ref[...])
    x1_ref[...] = x1
    xb = x1.astype(MXU_DTYPE)
    x1b_ref[...] = xb
    hid = _silu(_dot(xb, sw1_ref[...])) * _dot(xb, sw3_ref[...])
    sh_ref[...] = _dot(hid.astype(MXU_DTYPE), sw2_ref[...])
    lg_ref[...] = _dot_nt(rw_ref[...], xb)


def _mid_block(o, x, w_o, ln_g, ln_b, sw1, sw3, sw2, router_w, alpha):
    t, d = x.shape
    f = sw1.shape[1]
    e = router_w.shape[1]
    tm = _pick_tm(t, 512)
    row = lambda n: pl.BlockSpec((tm, n), lambda i: (i, 0))
    res = lambda a, c: pl.BlockSpec((a, c), lambda i: (0, 0))
    return pl.pallas_call(
        functools.partial(_mid_kernel, alpha=alpha),
        out_shape=(jax.ShapeDtypeStruct((t, d), jnp.float32),
                   jax.ShapeDtypeStruct((t, d), MXU_DTYPE),
                   jax.ShapeDtypeStruct((t, d), jnp.float32),
                   jax.ShapeDtypeStruct((e, t), jnp.float32)),
        grid=(t // tm,),
        in_specs=[row(o.shape[1]), row(d), res(o.shape[1], d), res(1, d), res(1, d),
                  res(d, f), res(d, f), res(f, d), res(e, d)],
        out_specs=(row(d), row(d), row(d), pl.BlockSpec((e, tm), lambda i: (0, i))),
        compiler_params=_cparams("parallel"),
        name="mixer_out_ln_shared_router",
    )(o, x, w_o.astype(MXU_DTYPE), ln_g.reshape(1, d), ln_b.reshape(1, d),
      sw1.astype(MXU_DTYPE), sw3.astype(MXU_DTYPE), sw2.astype(MXU_DTYPE), router_w.T.astype(MXU_DTYPE))


def _expert_kernel(blk_e_ref, xs_ref, w1_ref, w3_ref, w2_ref, y_ref):
    xe = xs_ref[...]
    hid = _silu(_dot(xe, w1_ref[0])) * _dot(xe, w3_ref[0])
    y_ref[...] = _dot(hid.astype(MXU_DTYPE), w2_ref[0]).astype(y_ref.dtype)


def _grouped_experts(xs, blk_e, w1, w3, w2):
    n_slots, d = xs.shape
    f = w1.shape[2]
    n_blocks = n_slots // EXPERT_BLOCK
    return pl.pallas_call(
        _expert_kernel,
        out_shape=jax.ShapeDtypeStruct((n_slots, d), MXU_DTYPE),
        grid_spec=pltpu.PrefetchScalarGridSpec(
            num_scalar_prefetch=1,
            grid=(n_blocks,),
            in_specs=[pl.BlockSpec((EXPERT_BLOCK, d), lambda i, be: (i, 0)),
                      pl.BlockSpec((1, d, f), lambda i, be: (be[i], 0, 0)),
                      pl.BlockSpec((1, d, f), lambda i, be: (be[i], 0, 0)),
                      pl.BlockSpec((1, f, d), lambda i, be: (be[i], 0, 0))],
            out_specs=pl.BlockSpec((EXPERT_BLOCK, d), lambda i, be: (i, 0)),
        ),
        compiler_params=_cparams("arbitrary"),
        name="routed_experts",
    )(blk_e, xs, w1.astype(MXU_DTYPE), w3.astype(MXU_DTYPE), w2.astype(MXU_DTYPE))


ROUTER_TM = 512
MOE_PARTS = 2


def _pick_rows(rows, row_id, n):
    out = jnp.zeros((n,) + rows[0].shape[1:], rows[0].dtype)
    for r in range(n):
        out = jnp.where(row_id == r, rows[r], out)
    return out


def _router_kernel(lg_ref, bias_ref, tri_ref, eidx_ref, gate_ref, rank_ref, cnt_ref, carry_ref):
    e, tm = lg_ref.shape
    per_group = e // N_GROUPS

    @pl.when(pl.program_id(0) == 0)
    def _():
        carry_ref[...] = jnp.zeros(carry_ref.shape, jnp.float32)

    s = _sigmoid(lg_ref[...])
    sb = s + bias_ref[...]
    neg_inf = -jnp.inf

    sub_id = lax.broadcasted_iota(jnp.int32, (per_group, tm), 0)
    g_rows = []
    for g in range(N_GROUPS):
        blk = sb[g * per_group:(g + 1) * per_group, :]
        m1 = jnp.max(blk, axis=0, keepdims=True)
        f1 = jnp.min(jnp.where(blk == m1, sub_id, per_group), axis=0, keepdims=True)
        m2 = jnp.max(jnp.where(sub_id == f1, neg_inf, blk), axis=0, keepdims=True)
        g_rows.append(m1 + m2)
    g_id = lax.broadcasted_iota(jnp.int32, (N_GROUPS, tm), 0)
    gscore = _pick_rows(g_rows, g_id, N_GROUPS)

    e_id = lax.broadcasted_iota(jnp.int32, (e, tm), 0)
    e_group = e_id // per_group if per_group & (per_group - 1) else e_id >> (per_group.bit_length() - 1)
    g_taken = jnp.zeros((N_GROUPS, tm), jnp.bool_)
    e_allowed = jnp.zeros((e, tm), jnp.bool_)
    for _ in range(TOPK_GROUPS):
        cand = jnp.where(g_taken, neg_inf, gscore)
        best = jnp.max(cand, axis=0, keepdims=True)
        first = jnp.min(jnp.where(cand == best, g_id, N_GROUPS), axis=0, keepdims=True)
        g_taken = g_taken | (g_id == first)
        e_allowed = e_allowed | (e_group == first)

    masked = jnp.where(e_allowed, sb, NEG)
    chosen = jnp.zeros((e, tm), jnp.bool_)
    id_rows, sel_rows = [], []
    for _ in range(TOP_K):
        cand = jnp.where(chosen, neg_inf, masked)
        best = jnp.max(cand, axis=0, keepdims=True)
        first = jnp.min(jnp.where(cand == best, e_id, e), axis=0, keepdims=True)
        hit = e_id == first
        chosen = chosen | hit
        id_rows.append(first)
        sel_rows.append(jnp.sum(jnp.where(hit, s, 0.0), axis=0, keepdims=True))
    k_id = lax.broadcasted_iota(jnp.int32, (TOP_K, tm), 0)
    eidx = _pick_rows(id_rows, k_id, TOP_K)
    sel = _pick_rows(sel_rows, k_id, TOP_K)
    eidx_ref[...] = eidx
    gate_ref[...] = sel / jnp.sum(sel, axis=0, keepdims=True) * ROUTED_SCALE

    chosen_f = jnp.where(chosen, 1.0, 0.0)
    incl = _dot(chosen_f.astype(MXU_DTYPE), tri_ref[...])
    rank_dense = carry_ref[...] + incl - chosen_f
    rank_rows = [jnp.sum(jnp.where(e_id == id_rows[r], rank_dense, 0.0), axis=0, keepdims=True)
                 for r in range(TOP_K)]
    rank_ref[...] = _pick_rows(rank_rows, k_id, TOP_K).astype(jnp.int32)
    carry_ref[...] = carry_ref[...] + jnp.sum(chosen_f, axis=1, keepdims=True)
    cnt_ref[...] = carry_ref[...]


def _route(logits_t, router_bias, part, n_parts):
    e, t_all = logits_t.shape
    t = t_all // n_parts
    tm = _pick_tm(t, ROUTER_TM)
    off = part * (t // tm)
    tri = jnp.asarray(np.triu(np.ones((tm, tm), np.float32)), dtype=MXU_DTYPE)
    kt = lambda dt: jax.ShapeDtypeStruct((TOP_K, t), dt)
    col = pl.BlockSpec((TOP_K, tm), lambda i: (0, i))
    eidx, gates, rank, cnt = pl.pallas_call(
        _router_kernel,
        out_shape=(kt(jnp.int32), kt(jnp.float32), kt(jnp.int32), jax.ShapeDtypeStruct((e, 1), jnp.float32)),
        grid=(t // tm,),
        in_specs=[pl.BlockSpec((e, tm), lambda i: (0, i + off)),
                  pl.BlockSpec((e, 1), lambda i: (0, 0)),
                  pl.BlockSpec((tm, tm), lambda i: (0, 0))],
        out_specs=(col, col, col, pl.BlockSpec((e, 1), lambda i: (0, 0))),
        scratch_shapes=[pltpu.VMEM((e, 1), jnp.float32)],
        compiler_params=_cparams("arbitrary"),
        name="moe_router",
    )(logits_t, router_bias.astype(jnp.float32).reshape(e, 1), tri)
    return eidx, gates, rank, cnt[:, 0].astype(jnp.int32)


def _moe_routed(x1b, eidx, rank, counts, w1, w3, w2, tok_offset):
    n_tok = eidx.shape[1]
    tk = n_tok * TOP_K
    n_blocks = (tk + N_EXPERTS * (EXPERT_BLOCK - 1)) // EXPERT_BLOCK + 1
    n_slots = n_blocks * EXPERT_BLOCK
    padded = (counts + EXPERT_BLOCK - 1) // EXPERT_BLOCK * EXPERT_BLOCK
    e_ids = jnp.arange(N_EXPERTS, dtype=jnp.int32)
    pad_end = jnp.sum(jnp.where(e_ids[None, :] <= e_ids[:, None], padded[None, :], 0), axis=1)
    start_pad = (pad_end - padded).astype(jnp.int32)
    start_of = jnp.sum(jnp.where(eidx[None] == e_ids[:, None, None], start_pad[:, None, None], 0), axis=0)
    dest = start_of + rank
    tok1 = jnp.broadcast_to(jnp.arange(n_tok, dtype=jnp.int32)[None, :] + 1, (TOP_K, n_tok))
    hit = jnp.zeros((n_slots,), jnp.int32).at[dest.reshape(tk)].add(tok1.reshape(tk), unique_indices=True)
    slot_tok = jnp.where(hit > 0, hit - 1, jnp.arange(n_slots, dtype=jnp.int32) % n_tok) + tok_offset
    blk_start = jnp.arange(n_blocks, dtype=jnp.int32) * EXPERT_BLOCK
    blk_e = jnp.minimum(jnp.sum((pad_end[None, :] <= blk_start[:, None]).astype(jnp.int32), axis=1),
                        N_EXPERTS - 1).astype(jnp.int32)
    y = _grouped_experts(x1b[slot_tok], blk_e, w1, w3, w2)
    return y, dest


def _post_kernel(x1_ref, sh_ref, yg_ref, gt_ref, p_ref, g_ref, b_ref, wp_ref, ng_ref, wg_ref, o_ref, *, alpha):
    gates = gt_ref[...]
    routed = yg_ref[0].astype(jnp.float32) * gates[:, 0:1]
    for k in range(1, TOP_K):
        routed = routed + yg_ref[k].astype(jnp.float32) * gates[:, k:k + 1]
    y = alpha * x1_ref[...] + (sh_ref[...] + routed)
    x2 = _layer_norm(y, g_ref[...], b_ref[...])
    e = _dot(p_ref[...].astype(MXU_DTYPE), wp_ref[...])
    e = e * lax.rsqrt(jnp.mean(e * e, axis=-1, keepdims=True) + RMS_EPS) * ng_ref[...]
    gate = _sigmoid(_dot(x2.astype(MXU_DTYPE), wg_ref[...]))
    o_ref[...] = x2 + gate * e


def _post_block(x1, shared, y_pairs, gates_t, p, ln_g, ln_b, w_proj, norm_g, w_gate, alpha, part, n_parts):
    t_all, d = x1.shape
    t = t_all // n_parts
    pd = p.shape[1]
    tm = _pick_tm(t, 256)
    off = part * (t // tm)
    row = lambda n: pl.BlockSpec((tm, n), lambda i: (i, 0))
    full_row = lambda n: pl.BlockSpec((tm, n), lambda i: (i + off, 0))
    res = lambda a, c: pl.BlockSpec((a, c), lambda i: (0, 0))
    return pl.pallas_call(
        functools.partial(_post_kernel, alpha=alpha),
        out_shape=jax.ShapeDtypeStruct((t_all, d), jnp.float32),
        grid=(t // tm,),
        in_specs=[full_row(d), full_row(d), pl.BlockSpec((TOP_K, tm, d), lambda i: (0, i, 0)), row(TOP_K),
                  full_row(pd), res(1, d), res(1, d), res(pd, d), res(1, d), res(d, d)],
        out_specs=full_row(d),
        input_output_aliases={0: 0},
        compiler_params=_cparams("parallel"),
        name="moe_combine_ln_ple",
    )(x1, shared, y_pairs, gates_t, p, ln_g.reshape(1, d), ln_b.reshape(1, d), w_proj.astype(MXU_DTYPE),
      norm_g.reshape(1, d), w_gate.astype(MXU_DTYPE))


def _rope_tables(s_len):
    inv = 1.0 / (ROPE_THETA ** (np.arange(0, HEAD_DIM, 2, dtype=np.float32) / HEAD_DIM))
    ang = jnp.arange(s_len, dtype=jnp.float32)[:, None] * jnp.asarray(inv, dtype=jnp.float32)[None, :]
    return jnp.cos(ang), jnp.sin(ang)


def _rope(x, cos, sin):
    b, l, w = x.shape
    xh = x.reshape(b, l, w // HEAD_DIM, HEAD_DIM)
    half = HEAD_DIM // 2
    x1, x2 = xh[..., :half], xh[..., half:]
    c = cos[:, None, :]
    s = sin[:, None, :]
    return jnp.concatenate([x1 * c - x2 * s, x2 * c + x1 * s], axis=-1).reshape(b, l, w)


def _split_cols(h, sizes):
    out, c = [], 0
    for n in sizes:
        out.append(h[..., c:c + n])
        c += n
    return out


def _even_mixer(x, w_in, b_gate, pos_k, w1_k, w2_k, pos_v, w1_v, w2_v, cos, sin):
    b, s, d = x.shape
    mix = A_HEADS * HEAD_DIM
    q_a, q_i, q_b, rest = _even_projection(x.reshape(b * s, d), w_in, s)
    q_a, q_i, q_b = (q.reshape(b, s, mix) for q in (q_a, q_i, q_b))
    k_a, v_a, k_i, w_i, kv_b, g_b = _split_cols(rest.reshape(b, s, -1),
                                                (HEAD_DIM, HEAD_DIM, IDX_DIM, IDX_HEADS, 6 * HEAD_DIM, 3 * B_HEADS))
    o_a = _dsa_attention(q_a, _rope(k_a, cos, sin), v_a, q_i, _rope(k_i, cos, sin), w_i)
    k_c, v_c, k_s, v_s, k_w, v_w = _split_cols(kv_b, (HEAD_DIM,) * 6)
    nc = s // CMP_STRIDE
    c_last = jnp.minimum(jnp.arange(nc) * CMP_STRIDE + CMP_LEN - 1, s - 1)
    kc = _rope(_compress(k_c, pos_k, w1_k, w2_k), cos[c_last], sin[c_last])
    vc = _compress(v_c, pos_v, w1_v, w2_v)
    gate = (g_b + b_gate).reshape(b, s, B_HEADS, 3)
    o_b = _nsa_attention(q_b, kc, vc, _rope(k_s, cos, sin), v_s, _rope(k_w, cos, sin), v_w, gate)
    return jnp.concatenate([o_a, o_b], axis=-1).reshape(b * s, 2 * mix)


def _odd_mixer(x, w_in, b_f):
    b, s, d = x.shape
    mix = C_HEADS * HEAD_DIM
    h = _matmul(x.reshape(b * s, d), w_in, name="odd_in_proj").reshape(b, s, -1)
    f = h[..., 3 * mix:3 * mix + C_HEADS]
    log_f = jax.nn.log_sigmoid((f + b_f).astype(jnp.float32))
    return _fox_attention(h, log_f).reshape(b * s, mix)


def kernel(x, p, ev_w_in, ev_b_gate, ev_cmp_pos_k, ev_cmp_w1_k, ev_cmp_w2_k, ev_cmp_pos_v, ev_cmp_w1_v, ev_cmp_w2_v, ev_w_o, od_w_in, od_b_f, od_w_o, ln1_g, ln1_b, ln2_g, ln2_b, router_w, router_bias, exp_w1, exp_w3, exp_w2, sh_w1, sh_w3, sh_w2, ple_w_gate, ple_w_proj, ple_norm_g):
    b, s, d = x.shape
    depth = p.shape[0]
    alpha = float((2.0 * depth) ** 0.25)
    cos, sin = _rope_tables(s)
    xf = x.reshape(b * s, d)
    for i in range(depth):
        j = i // 2
        x3 = xf.reshape(b, s, d)
        if i % 2 == 0:
            o = _even_mixer(x3, ev_w_in[j], ev_b_gate[j], ev_cmp_pos_k[j], ev_cmp_w1_k[j], ev_cmp_w2_k[j],
                            ev_cmp_pos_v[j], ev_cmp_w1_v[j], ev_cmp_w2_v[j], cos, sin)
            w_o = ev_w_o[j]
        else:
            o = _odd_mixer(x3, od_w_in[j], od_b_f[j])
            w_o = od_w_o[j]
        x1, x1b, shared, logits_t = _mid_block(o, xf, w_o, ln1_g[i], ln1_b[i], sh_w1[i], sh_w3[i], sh_w2[i],
                                               router_w[i], alpha)
        xf = x1
        for part in range(MOE_PARTS):
            eidx, gates, rank, counts = _route(logits_t, router_bias[i], part, MOE_PARTS)
            y, dest = _moe_routed(x1b, eidx, rank, counts, exp_w1[i], exp_w3[i], exp_w2[i],
                                  part * (b * s // MOE_PARTS))
            xf = _post_block(xf, shared, y[dest], gates.T, p[i].reshape(b * s, -1), ln2_g[i], ln2_b[i],
                             ple_w_proj[i], ple_norm_g[i], ple_w_gate[i], alpha, part, MOE_PARTS)
    return xf.reshape(b, s, d)
```
